```python
import math
import jax, jax.numpy as jnp
from jax import lax
import numpy as np

D_MODEL = 1024
BATCH = 8
SEQ = 2048
DEPTH = 4
DEC_BATCH = 128
DEC_SEQ = 8
PAST_LEN = 16384
PAGE_SIZE = 128

N_MIXERS = 4
ALPHA = (2 * DEPTH) ** 0.25
BETA = (8 * DEPTH) ** -0.25
LN_EPS = 1e-5
RMS_EPS = 1e-5

S5_WIDTH = D_MODEL
S5_GROUP = 16
S5_GROUPS = S5_WIDTH // S5_GROUP
S5_STATE = 64

POOL_WIDTH = D_MODEL
POOL_WINDOWS = (2, 4, 8, 16)
POOL_GROUP = POOL_WIDTH // len(POOL_WINDOWS)
POOL_BUF = max(POOL_WINDOWS) - 1

CMLP_WIDTH = D_MODEL
CMLP_CHUNK = 128
CMLP_HEADS = 4
CMLP_HEAD_DIM = CMLP_WIDTH // CMLP_HEADS

SSD_INNER = 2 * D_MODEL
SSD_HEAD_DIM = 64
SSD_HEADS = SSD_INNER // SSD_HEAD_DIM
SSD_STATE = 128
SSD_GROUPS = 4
SSD_CONV = 4
SSD_CHUNK = 128
SSD_CONV_DIM = SSD_INNER + 2 * SSD_GROUPS * SSD_STATE
SSD_PROJ = SSD_INNER + SSD_CONV_DIM + SSD_HEADS

PEER_HEADS = 8
PEER_NKEYS = 128
PEER_EXPERTS = PEER_NKEYS * PEER_NKEYS
PEER_TOPK = 16
PEER_QDIM = 256
PEER_HALF = PEER_QDIM // 2
PEER_BLOCK = 128

kernel_name = 'hybrid_s5_pool_gmlp_ssd_peer_step'


def layer_norm(x, g, b):
    xf = x.astype(jnp.float32)
    mu = jnp.mean(xf, axis=-1, keepdims=True)
    var = jnp.mean(jnp.square(xf - mu), axis=-1, keepdims=True)
    y = (xf - mu) * lax.rsqrt(var + LN_EPS) * g.astype(jnp.float32) + b.astype(jnp.float32)
    return y.astype(x.dtype)


def _cmul(ar, ai, br, bi):
    return ar * br - ai * bi, ar * bi + ai * br


def _s5_combine(e1, e2):
    a1r, a1i, b1r, b1i = e1
    a2r, a2i, b2r, b2i = e2
    ar, ai = _cmul(a2r, a2i, a1r, a1i)
    br, bi = _cmul(a2r, a2i, b1r, b1i)
    return ar, ai, br + b2r, bi + b2i


def s5_mixer(x, h0_re, h0_im, w_in, a_re, a_im, log_dt, b_re, b_im, c_re, c_im, d_skip,
             w_glu, b_glu, w_out):
    f32 = jnp.float32
    bsz, L, _ = x.shape
    u = (x @ w_in).astype(f32)
    ug = u.reshape(bsz, L, S5_GROUPS, S5_GROUP)
    dt = jnp.exp(log_dt.astype(f32))[:, None]
    lam_r, lam_i = a_re.astype(f32), a_im.astype(f32)
    mag = jnp.exp(lam_r * dt)
    lb_r, lb_i = mag * jnp.cos(lam_i * dt), mag * jnp.sin(lam_i * dt)
    den = lam_r * lam_r + lam_i * lam_i
    f_r = ((lb_r - 1.0) * lam_r + lb_i * lam_i) / den
    f_i = (lb_i * lam_r - (lb_r - 1.0) * lam_i) / den
    bb_r, bb_i = _cmul(f_r[..., None], f_i[..., None], b_re.astype(f32), b_im.astype(f32))
    bu_r = jnp.einsum('blgi,gpi->blgp', ug, bb_r)
    bu_i = jnp.einsum('blgi,gpi->blgp', ug, bb_i)
    h0r, h0i = h0_re.astype(f32), h0_im.astype(f32)
    bu_r = bu_r.at[:, 0].add(lb_r * h0r - lb_i * h0i)
    bu_i = bu_i.at[:, 0].add(lb_r * h0i + lb_i * h0r)
    a_r = jnp.broadcast_to(lb_r, bu_r.shape)
    a_i = jnp.broadcast_to(lb_i, bu_i.shape)
    _, _, h_r, h_i = lax.associative_scan(_s5_combine, (a_r, a_i, bu_r, bu_i), axis=1)
    y = (jnp.einsum('blgp,gip->blgi', h_r, c_re.astype(f32))
         - jnp.einsum('blgp,gip->blgi', h_i, c_im.astype(f32)))
    y = y.reshape(bsz, L, S5_WIDTH) + d_skip.astype(f32) * u
    g = jax.nn.gelu(y).astype(x.dtype)
    out = g * jax.nn.sigmoid(g @ w_glu + b_glu)
    return out @ w_out, h_r[:, -1], h_i[:, -1]


def pool_mixer(x, buf, pos0, w_in, w_grp, scale, w_out):
    f32 = jnp.float32
    bsz, L, _ = x.shape
    u = x @ w_in
    ctx = jnp.concatenate([buf.astype(u.dtype), u], axis=1)
    cs = jnp.pad(jnp.cumsum(ctx.astype(f32), axis=1), ((0, 0), (1, 0), (0, 0)))
    pos = pos0 + jnp.arange(L)
    means = []
    for gi, w in enumerate(POOL_WINDOWS):
        c0, c1 = gi * POOL_GROUP, (gi + 1) * POOL_GROUP
        win_sum = (cs[:, POOL_BUF + 1:POOL_BUF + 1 + L, c0:c1]
                   - cs[:, POOL_BUF + 1 - w:POOL_BUF + 1 - w + L, c0:c1])
        cnt = jnp.minimum(pos + 1, w).astype(f32)[None, :, None]
        means.append(win_sum / cnt)
    pooled = (jnp.concatenate(means, axis=-1) - u.astype(f32)).reshape(
        bsz, L, len(POOL_WINDOWS), POOL_GROUP)
    mixed = jnp.einsum('blgc,gcd->blgd', pooled, w_grp.astype(f32)).reshape(bsz, L, POOL_WIDTH)
    out = (mixed * scale.astype(f32)).astype(x.dtype)
    return out @ w_out, ctx[:, -POOL_BUF:]


def chunk_mlp_mixer(x, w_in, b_in, ln_g, ln_b, w_s, b_s, w_out):
    bsz, L, _ = x.shape
    z = jax.nn.gelu(x @ w_in + b_in)
    u, v = z[..., :CMLP_WIDTH], z[..., CMLP_WIDTH:]
    v = layer_norm(v, ln_g, ln_b)
    q = min(L, CMLP_CHUNK)
    n_chunks = L // q
    causal = jnp.tril(jnp.ones((q, q), dtype=bool))
    ws = jnp.where(causal[None], w_s[:, :q, :q], 0.0).astype(v.dtype)
    vc = v.reshape(bsz, n_chunks, q, CMLP_HEADS, CMLP_HEAD_DIM)
    mixed = (jnp.einsum('hts,bcshd->bcthd', ws, vc)
             + jnp.transpose(b_s[:, :q])[None, None, :, :, None])
    out = u * mixed.reshape(bsz, L, CMLP_WIDTH)
    return out @ w_out, v


def ssd_scan(xs, dt, a, bm, cm, h0, q):
    f32 = jnp.float32
    bsz, L, _, _ = xs.shape
    nc = L // q
    hpg = SSD_HEADS // SSD_GROUPS
    x = xs.astype(f32).reshape(bsz, nc, q, SSD_GROUPS, hpg, SSD_HEAD_DIM)
    dtc = dt.reshape(bsz, nc, q, SSD_GROUPS, hpg)
    bc = bm.astype(f32).reshape(bsz, nc, q, SSD_GROUPS, SSD_STATE)
    cc = cm.astype(f32).reshape(bsz, nc, q, SSD_GROUPS, SSD_STATE)
    a_cs = jnp.cumsum(dtc * a.reshape(SSD_GROUPS, hpg), axis=2)
    causal = jnp.tril(jnp.ones((q, q), dtype=bool))[:, :, None, None]
    seg = a_cs[:, :, :, None] - a_cs[:, :, None, :]
    decay = jnp.exp(jnp.where(causal, seg, -jnp.inf))
    xdt = x * dtc[..., None]
    cb = jnp.einsum('bctgn,bcsgn->bctsg', cc, bc)
    y_diag = jnp.einsum('bctsg,bctsgj,bcsgjp->bctgjp', cb, decay, xdt)
    decay_end = jnp.exp(a_cs[:, :, -1:] - a_cs)
    states = jnp.einsum('bcsgn,bcsgj,bcsgjp->bcgjpn', bc, decay_end, xdt)
    chunk_decay = jnp.exp(a_cs[:, :, -1])

    def step(h, inp):
        st, dec = inp
        return h * dec[..., None, None] + st, h

    h_init = h0.astype(f32).reshape(bsz, SSD_GROUPS, hpg, SSD_HEAD_DIM, SSD_STATE)
    h_last, h_prev = lax.scan(step, h_init,
                              (jnp.moveaxis(states, 1, 0), jnp.moveaxis(chunk_decay, 1, 0)))
    h_prev = jnp.moveaxis(h_prev, 0, 1)
    y_off = jnp.einsum('bctgn,bcgjpn,bctgj->bctgjp', cc, h_prev, jnp.exp(a_cs))
    y = (y_diag + y_off).reshape(bsz, L, SSD_HEADS, SSD_HEAD_DIM)
    return y, h_last.reshape(bsz, SSD_HEADS, SSD_HEAD_DIM, SSD_STATE)


def ssd_mixer(x, conv_buf, h0, w_in, conv_w, conv_b, dt_bias, a_log, d_skip, norm_g, w_out):
    f32 = jnp.float32
    bsz, L, _ = x.shape
    proj = x @ w_in
    z = proj[..., :SSD_INNER]
    xbc = proj[..., SSD_INNER:SSD_INNER + SSD_CONV_DIM]
    dt_raw = proj[..., SSD_INNER + SSD_CONV_DIM:]
    ctx = jnp.concatenate([conv_buf.astype(xbc.dtype), xbc], axis=1)
    conv = conv_b
    for k in range(SSD_CONV):
        conv = conv + ctx[:, k:k + L] * conv_w[k]
    xbc = jax.nn.silu(conv)
    gn = SSD_GROUPS * SSD_STATE
    xs = xbc[..., :SSD_INNER].reshape(bsz, L, SSD_HEADS, SSD_HEAD_DIM)
    bm = xbc[..., SSD_INNER:SSD_INNER + gn].reshape(bsz, L, SSD_GROUPS, SSD_STATE)
    cm = xbc[..., SSD_INNER + gn:].reshape(bsz, L, SSD_GROUPS, SSD_STATE)
    dt = jax.nn.softplus(dt_raw.astype(f32) + dt_bias.astype(f32))
    a = -jnp.exp(a_log.astype(f32))
    q = SSD_CHUNK if L % SSD_CHUNK == 0 else L
    y, h_last = ssd_scan(xs, dt, a, bm, cm, h0, q)
    y = y + d_skip.astype(f32)[:, None] * xs.astype(f32)
    yg = (y.reshape(bsz, L, SSD_INNER) * jax.nn.silu(z.astype(f32))).reshape(
        bsz, L, SSD_GROUPS, SSD_INNER // SSD_GROUPS)
    yg = yg * lax.rsqrt(jnp.mean(jnp.square(yg), axis=-1, keepdims=True) + RMS_EPS)
    y = (yg.reshape(bsz, L, SSD_INNER) * norm_g.astype(f32)).astype(x.dtype)
    return y @ w_out, ctx[:, -(SSD_CONV - 1):], h_last


def peer_ffn(x, w_q, sub_keys, expert_u, expert_v):
    f32 = jnp.float32
    shp = x.shape
    xt = x.reshape(-1, D_MODEL)
    T = xt.shape[0]
    nb = -(-T // PEER_BLOCK)
    xt = jnp.pad(xt, ((0, nb * PEER_BLOCK - T), (0, 0))).reshape(nb, PEER_BLOCK, D_MODEL)
    keys = sub_keys.astype(f32)

    def block(xb):
        q = (xb @ w_q).astype(f32).reshape(PEER_BLOCK, PEER_HEADS, 2, PEER_HALF)
        s = jnp.einsum('thid,hind->thin', q, keys)
        top_s, top_i = lax.top_k(s, PEER_TOPK)
        cand_s = top_s[:, :, 0, :, None] + top_s[:, :, 1, None, :]
        cand_i = top_i[:, :, 0, :, None] * PEER_NKEYS + top_i[:, :, 1, None, :]
        best_s, best_j = lax.top_k(cand_s.reshape(PEER_BLOCK, PEER_HEADS, -1), PEER_TOPK)
        idx = jnp.take_along_axis(cand_i.reshape(PEER_BLOCK, PEER_HEADS, -1), best_j, axis=-1)
        gate = jax.nn.softmax(best_s, axis=-1)
        u = expert_u[idx]
        v = expert_v[idx]
        act = jax.nn.gelu(jnp.einsum('td,thkd->thk', xb, u).astype(f32))
        return jnp.einsum('thk,thkd->td', (gate * act).astype(xb.dtype), v)

    out = lax.map(block, xt)
    return out.reshape(-1, D_MODEL)[:T].reshape(shp)


def trunk(h, pos0, s5_h_re, s5_h_im, pool_buf, ssd_conv_buf, ssd_h,
          s5_w_in, s5_a_re, s5_a_im, s5_log_dt, s5_b_re, s5_b_im, s5_c_re, s5_c_im, s5_d,
          s5_w_glu, s5_b_glu, s5_w_out,
          pool_w_in, pool_w_grp, pool_scale, pool_w_out,
          cmlp_w_in, cmlp_b_in, cmlp_ln_g, cmlp_ln_b, cmlp_w_s, cmlp_b_s, cmlp_w_out,
          ssd_w_in, ssd_conv_w, ssd_conv_b, ssd_dt_bias, ssd_a_log, ssd_d, ssd_norm_g, ssd_w_out,
          ln1_g, ln1_b, ln2_g, ln2_b, peer_w_q, peer_keys, peer_u, peer_v):
    cmlp_v = None
    for i in range(DEPTH):
        kind = i % N_MIXERS
        if kind == 0:
            mix, s5_h_re, s5_h_im = s5_mixer(h, s5_h_re, s5_h_im, s5_w_in, s5_a_re, s5_a_im,
                                             s5_log_dt, s5_b_re, s5_b_im, s5_c_re, s5_c_im,
                                             s5_d, s5_w_glu, s5_b_glu, s5_w_out)
        elif kind == 1:
            mix, pool_buf = pool_mixer(h, pool_buf, pos0, pool_w_in, pool_w_grp, pool_scale,
                                       pool_w_out)
        elif kind == 2:
            mix, cmlp_v = chunk_mlp_mixer(h, cmlp_w_in, cmlp_b_in, cmlp_ln_g, cmlp_ln_b,
                                          cmlp_w_s, cmlp_b_s, cmlp_w_out)
        else:
            mix, ssd_conv_buf, ssd_h = ssd_mixer(h, ssd_conv_buf, ssd_h, ssd_w_in, ssd_conv_w,
                                                 ssd_conv_b, ssd_dt_bias, ssd_a_log, ssd_d,
                                                 ssd_norm_g, ssd_w_out)
        h = layer_norm(ALPHA * h + mix, ln1_g[i], ln1_b[i])
        ffn = peer_ffn(h, peer_w_q[i], peer_keys[i], peer_u[i], peer_v[i])
        h = layer_norm(ALPHA * h + ffn, ln2_g[i], ln2_b[i])
    return h, s5_h_re, s5_h_im, pool_buf, cmlp_v, ssd_conv_buf, ssd_h


def setup_inputs(seed: int = 0) -> dict:
    key = jax.random.key(seed)
    ks = iter(jax.random.split(key, 64))

    def nrm(shape, scale=1.0):
        return jax.random.normal(next(ks), shape, jnp.float32) * scale

    def unif(shape, lo, hi):
        return jax.random.uniform(next(ks), shape, jnp.float32, lo, hi)

    p = {}
    p['x_prompt'] = nrm((BATCH, SEQ, D_MODEL))
    p['x_sample'] = nrm((DEC_BATCH, DEC_SEQ, D_MODEL))
    p['state_s5_re'] = nrm((DEC_BATCH, S5_GROUPS, S5_STATE), 0.2)
    p['state_s5_im'] = nrm((DEC_BATCH, S5_GROUPS, S5_STATE), 0.2)
    p['state_pool'] = nrm((DEC_BATCH, POOL_BUF, POOL_WIDTH))
    p['state_ssd_conv'] = nrm((DEC_BATCH, SSD_CONV - 1, SSD_CONV_DIM))
    p['state_ssd'] = nrm((DEC_BATCH, SSD_HEADS, SSD_HEAD_DIM, SSD_STATE), 0.5)
    n_idx = jnp.arange(S5_STATE, dtype=jnp.float32)[None, :]
    p['s5_w_in'] = nrm((D_MODEL, S5_WIDTH), D_MODEL ** -0.5)
    p['s5_a_re'] = -0.5 + nrm((S5_GROUPS, S5_STATE), 0.01)
    p['s5_a_im'] = math.pi * n_idx + nrm((S5_GROUPS, S5_STATE), 0.01)
    p['s5_log_dt'] = unif((S5_GROUPS,), math.log(1e-3), math.log(1e-1))
    p['s5_b_re'] = nrm((S5_GROUPS, S5_STATE, S5_GROUP), (2 * S5_GROUP) ** -0.5)
    p['s5_b_im'] = nrm((S5_GROUPS, S5_STATE, S5_GROUP), (2 * S5_GROUP) ** -0.5)
    p['s5_c_re'] = nrm((S5_GROUPS, S5_GROUP, S5_STATE), S5_STATE ** -0.5)
    p['s5_c_im'] = nrm((S5_GROUPS, S5_GROUP, S5_STATE), S5_STATE ** -0.5)
    p['s5_d'] = nrm((S5_WIDTH,))
    p['s5_w_glu'] = nrm((S5_WIDTH, S5_WIDTH), S5_WIDTH ** -0.5)
    p['s5_b_glu'] = nrm((S5_WIDTH,), 0.01)
    p['s5_w_out'] = nrm((S5_WIDTH, D_MODEL), BETA * S5_WIDTH ** -0.5)
    p['pool_w_in'] = nrm((D_MODEL, POOL_WIDTH), D_MODEL ** -0.5)
    p['pool_w_grp'] = nrm((len(POOL_WINDOWS), POOL_GROUP, POOL_GROUP), POOL_GROUP ** -0.5)
    p['pool_scale'] = 1.0 + nrm((POOL_WIDTH,), 0.01)
    p['pool_w_out'] = nrm((POOL_WIDTH, D_MODEL), BETA * POOL_WIDTH ** -0.5)
    p['cmlp_w_in'] = nrm((D_MODEL, 2 * CMLP_WIDTH), D_MODEL ** -0.5)
    p['cmlp_b_in'] = nrm((2 * CMLP_WIDTH,), 0.01)
    p['cmlp_ln_g'] = 1.0 + nrm((CMLP_WIDTH,), 0.01)
    p['cmlp_ln_b'] = nrm((CMLP_WIDTH,), 0.01)
    p['cmlp_w_s'] = nrm((CMLP_HEADS, CMLP_CHUNK, CMLP_CHUNK), 0.5 * CMLP_CHUNK ** -0.5)
    p['cmlp_b_s'] = 1.0 + nrm((CMLP_HEADS, CMLP_CHUNK), 0.1)
    p['cmlp_w_out'] = nrm((CMLP_WIDTH, D_MODEL), BETA * CMLP_WIDTH ** -0.5)
    dt0 = jnp.exp(unif((SSD_HEADS,), math.log(1e-3), math.log(1e-1)))
    p['ssd_w_in'] = nrm((D_MODEL, SSD_PROJ), D_MODEL ** -0.5)
    p['ssd_conv_w'] = nrm((SSD_CONV, SSD_CONV_DIM), SSD_CONV ** -0.5)
    p['ssd_conv_b'] = nrm((SSD_CONV_DIM,), 0.01)
    p['ssd_dt_bias'] = dt0 + jnp.log(-jnp.expm1(-dt0))
    p['ssd_a_log'] = jnp.log(unif((SSD_HEADS,), 1.0, 16.0))
    p['ssd_d'] = 1.0 + nrm((SSD_HEADS,), 0.01)
    p['ssd_norm_g'] = 1.0 + nrm((SSD_INNER,), 0.01)
    p['ssd_w_out'] = nrm((SSD_INNER, D_MODEL), BETA * SSD_INNER ** -0.5)
    p['ln1_g'] = 1.0 + nrm((DEPTH, D_MODEL), 0.01)
    p['ln1_b'] = nrm((DEPTH, D_MODEL), 0.01)
    p['ln2_g'] = 1.0 + nrm((DEPTH, D_MODEL), 0.01)
    p['ln2_b'] = nrm((DEPTH, D_MODEL), 0.01)
    p['peer_w_q'] = nrm((DEPTH, D_MODEL, PEER_HEADS * PEER_QDIM), D_MODEL ** -0.5)
    p['peer_keys'] = nrm((DEPTH, PEER_HEADS, 2, PEER_NKEYS, PEER_HALF), PEER_HALF ** -0.5)
    p['peer_u'] = nrm((DEPTH, PEER_EXPERTS, D_MODEL), D_MODEL ** -0.5)
    p['peer_v'] = nrm((DEPTH, PEER_EXPERTS, D_MODEL), BETA * PEER_HEADS ** -0.5)
    return p


def reference(x_prompt, x_sample, state_s5_re, state_s5_im, state_pool, state_ssd_conv, state_ssd,
              s5_w_in, s5_a_re, s5_a_im, s5_log_dt, s5_b_re, s5_b_im, s5_c_re, s5_c_im, s5_d,
              s5_w_glu, s5_b_glu, s5_w_out,
              pool_w_in, pool_w_grp, pool_scale, pool_w_out,
              cmlp_w_in, cmlp_b_in, cmlp_ln_g, cmlp_ln_b, cmlp_w_s, cmlp_b_s, cmlp_w_out,
              ssd_w_in, ssd_conv_w, ssd_conv_b, ssd_dt_bias, ssd_a_log, ssd_d, ssd_norm_g, ssd_w_out,
              ln1_g, ln1_b, ln2_g, ln2_b, peer_w_q, peer_keys, peer_u, peer_v):
    weights = (s5_w_in, s5_a_re, s5_a_im, s5_log_dt, s5_b_re, s5_b_im, s5_c_re, s5_c_im, s5_d,
               s5_w_glu, s5_b_glu, s5_w_out,
               pool_w_in, pool_w_grp, pool_scale, pool_w_out,
               cmlp_w_in, cmlp_b_in, cmlp_ln_g, cmlp_ln_b, cmlp_w_s, cmlp_b_s, cmlp_w_out,
               ssd_w_in, ssd_conv_w, ssd_conv_b, ssd_dt_bias, ssd_a_log, ssd_d, ssd_norm_g, ssd_w_out,
               ln1_g, ln1_b, ln2_g, ln2_b, peer_w_q, peer_keys, peer_u, peer_v)
    f32 = jnp.float32
    bp = x_prompt.shape[0]
    (y_prompt, s5_re_p, s5_im_p, pool_p, _, conv_p, ssd_p) = trunk(
        x_prompt, 0,
        jnp.zeros((bp, S5_GROUPS, S5_STATE), f32),
        jnp.zeros((bp, S5_GROUPS, S5_STATE), f32),
        jnp.zeros((bp, POOL_BUF, POOL_WIDTH), x_prompt.dtype),
        jnp.zeros((bp, SSD_CONV - 1, SSD_CONV_DIM), x_prompt.dtype),
        jnp.zeros((bp, SSD_HEADS, SSD_HEAD_DIM, SSD_STATE), f32),
        *weights)
    (y_sample, s5_re_s, s5_im_s, pool_s, cmlp_v_s, conv_s, ssd_s) = trunk(
        x_sample, PAST_LEN, state_s5_re, state_s5_im, state_pool, state_ssd_conv, state_ssd,
        *weights)
    return (y_prompt, y_sample, s5_re_p, s5_im_p, pool_p, conv_p, ssd_p,
            s5_re_s, s5_im_s, pool_s, cmlp_v_s, conv_s, ssd_s)
```

```python
import functools
import math

import jax
import jax.numpy as jnp
from jax import lax
from jax.experimental import pallas as pl
from jax.experimental.pallas import tpu as pltpu

F32 = jnp.float32
BF16 = jnp.bfloat16

D_MODEL = 1024
DEPTH = 4
ALPHA = (2 * DEPTH) ** 0.25
LN_EPS = 1e-5
RMS_EPS = 1e-5

S5_GROUPS = 64
S5_GROUP = 16
S5_STATE = 64
S5_GB = 8
S5_ROWS = 1024

POOL_WINDOWS = (2, 4, 8, 16)
POOL_GROUP = 256
POOL_BUF = 15
POOL_HIST = 16

CMLP_WIDTH = 1024
CMLP_CHUNK = 128
CMLP_HEADS = 4
CMLP_HEAD_DIM = 256

SSD_INNER = 2048
SSD_HEAD_DIM = 64
SSD_HEADS = 32
SSD_STATE = 128
SSD_GROUPS = 4
SSD_HPG = SSD_HEADS // SSD_GROUPS
SSD_CONV = 4
SSD_CHUNK = 128
SSD_CONV_DIM = SSD_INNER + 2 * SSD_GROUPS * SSD_STATE
SSD_GN = SSD_GROUPS * SSD_STATE

PEER_HEADS = 8
PEER_NKEYS = 128
PEER_TOPK = 16
PEER_HALF = 128
PEER_ROUNDS = PEER_TOPK + 1
PEER_TQ = 256
PEER_TB = 512
PEER_EC = 1024

LANES = 128
SUBLANES = 8
VMEM_LIMIT_BYTES = 56 * 1024 * 1024


def _cparams(semantics, vmem=None):
    return pltpu.CompilerParams(dimension_semantics=semantics, vmem_limit_bytes=vmem)


def _ln(y, g, b):
    mu = jnp.mean(y, axis=-1, keepdims=True)
    yc = y - mu
    var = jnp.mean(yc * yc, axis=-1, keepdims=True)
    return yc * lax.rsqrt(var + LN_EPS) * g + b


def _dot(a, b):
    return jnp.dot(a, b, preferred_element_type=F32)


def _dot_nt(a, b):
    return lax.dot_general(a, b, (((1,), (1,)), ((), ())), preferred_element_type=F32)


def _dot_exact(a, b):
    return jnp.dot(a, b, preferred_element_type=F32, precision=lax.Precision.HIGHEST)


def _mm_kernel(*refs, act, has_bias, has_gate):
    x_ref, w_ref = refs[0], refs[1]
    k = 2
    acc = _dot(x_ref[...].astype(BF16), w_ref[...])
    if has_bias:
        acc = acc + refs[k][...]
        k += 1
    if act == "gelu":
        acc = jax.nn.gelu(acc)
    if has_gate:
        acc = refs[k][...] * jax.nn.sigmoid(acc)
        k += 1
    refs[k][...] = acc


def _mm(x, w, bias=None, act=None, gate=None, tm=512, tn=512):
    M, K = x.shape
    N = w.shape[1]
    tm, tn = min(tm, M), min(tn, N)
    assert M % tm == 0 and N % tn == 0
    in_specs = [pl.BlockSpec((tm, K), lambda i, j: (i, 0)),
                pl.BlockSpec((K, tn), lambda i, j: (0, j))]
    args = [x, w]
    if bias is not None:
        in_specs.append(pl.BlockSpec((1, tn), lambda i, j: (0, j)))
        args.append(bias.reshape(1, N))
    if gate is not None:
        in_specs.append(pl.BlockSpec((tm, tn), lambda i, j: (i, j)))
        args.append(gate)
    return pl.pallas_call(
        functools.partial(_mm_kernel, act=act, has_bias=bias is not None, has_gate=gate is not None),
        grid=(M // tm, N // tn),
        in_specs=in_specs,
        out_specs=pl.BlockSpec((tm, tn), lambda i, j: (i, j)),
        out_shape=jax.ShapeDtypeStruct((M, N), F32),
        compiler_params=_cparams(("parallel", "parallel"), VMEM_LIMIT_BYTES),
        name="mm",
    )(*args)


def _mm_ln_kernel(x_ref, w_ref, h_ref, g_ref, b_ref, o_ref):
    acc = _dot(x_ref[...].astype(BF16), w_ref[...])
    o_ref[...] = _ln(ALPHA * h_ref[...] + acc, g_ref[...], b_ref[...])


def _mm_ln(x, w, h, g, b, tm=256):
    M, K = x.shape
    N = w.shape[1]
    tm = min(tm, M)
    assert M % tm == 0
    return pl.pallas_call(
        _mm_ln_kernel,
        grid=(M // tm,),
        in_specs=[pl.BlockSpec((tm, K), lambda i: (i, 0)),
                  pl.BlockSpec((K, N), lambda i: (0, 0)),
                  pl.BlockSpec((tm, N), lambda i: (i, 0)),
                  pl.BlockSpec((1, N), lambda i: (0, 0)),
                  pl.BlockSpec((1, N), lambda i: (0, 0))],
        out_specs=pl.BlockSpec((tm, N), lambda i: (i, 0)),
        out_shape=jax.ShapeDtypeStruct((M, N), F32),
        compiler_params=_cparams(("parallel",), VMEM_LIMIT_BYTES),
        name="mm_ln",
    )(x, w, h, g.reshape(1, N), b.reshape(1, N))


def _s5_kernel(u_ref, wbb_ref, wc_ref, a_ref, d_ref, h0_ref, g_ref, hl_ref,
               h_scr, bu_scr, hs_scr, *, B, Tc):
    tc = pl.program_id(1)
    P = S5_GB * S5_STATE

    @pl.when(tc == 0)
    def _():
        h_scr[...] = h0_ref[0]

    u = u_ref[...]
    bu_scr[...] = _dot(u.astype(BF16), wbb_ref[0])
    ar = jnp.broadcast_to(a_ref[0, 0:1, :], (SUBLANES, P))
    ai = jnp.broadcast_to(a_ref[0, 1:2, :], (SUBLANES, P))

    def seq_body(sb, carry):
        r0 = pl.multiple_of(sb * SUBLANES, SUBLANES)

        def t_body(t, h):
            hr, hi = h
            rows = pl.ds(pl.multiple_of(t * B + r0, SUBLANES), SUBLANES)
            nr = ar * hr - ai * hi + bu_scr[rows, 0:P]
            ni = ar * hi + ai * hr + bu_scr[rows, P:2 * P]
            hs_scr[rows, 0:P] = nr
            hs_scr[rows, P:2 * P] = ni
            return nr, ni

        hr, hi = lax.fori_loop(0, Tc, t_body,
                               (h_scr[pl.ds(r0, SUBLANES), 0:P], h_scr[pl.ds(r0, SUBLANES), P:2 * P]))
        h_scr[pl.ds(r0, SUBLANES), 0:P] = hr
        h_scr[pl.ds(r0, SUBLANES), P:2 * P] = hi
        return carry

    lax.fori_loop(0, B // SUBLANES, seq_body, 0)
    y = _dot(hs_scr[...].astype(BF16), wc_ref[0]) + d_ref[...] * u
    g_ref[...] = jax.nn.gelu(y)

    @pl.when(tc == pl.num_programs(1) - 1)
    def _():
        hl_ref[0] = h_scr[...]


def _s5_core(u_tm, h0_blk, wbb, wc, a_blk, d_skip, B, L):
    Tc = S5_ROWS // B
    assert B % SUBLANES == 0 and L % Tc == 0
    P2 = 2 * S5_GB * S5_STATE
    W = S5_GB * S5_GROUP
    n_gb = S5_GROUPS // S5_GB
    return pl.pallas_call(
        functools.partial(_s5_kernel, B=B, Tc=Tc),
        grid=(n_gb, L // Tc),
        in_specs=[pl.BlockSpec((S5_ROWS, W), lambda g, t: (t, g)),
                  pl.BlockSpec((1, W, P2), lambda g, t: (g, 0, 0)),
                  pl.BlockSpec((1, P2, W), lambda g, t: (g, 0, 0)),
                  pl.BlockSpec((1, 2, P2 // 2), lambda g, t: (g, 0, 0)),
                  pl.BlockSpec((1, W), lambda g, t: (0, g)),
                  pl.BlockSpec((1, B, P2), lambda g, t: (g, 0, 0))],
        out_specs=[pl.BlockSpec((S5_ROWS, W), lambda g, t: (t, g)),
                   pl.BlockSpec((1, B, P2), lambda g, t: (g, 0, 0))],
        out_shape=[jax.ShapeDtypeStruct((L * B, D_MODEL), F32),
                   jax.ShapeDtypeStruct((n_gb, B, P2), F32)],
        scratch_shapes=[pltpu.VMEM((B, P2), F32),
                        pltpu.VMEM((S5_ROWS, P2), F32),
                        pltpu.VMEM((S5_ROWS, P2), F32)],
        compiler_params=_cparams(("parallel", "arbitrary"), VMEM_LIMIT_BYTES),
        name="s5_core",
    )(u_tm, wbb, wc, a_blk, d_skip.reshape(1, D_MODEL), h0_blk)


def _s5_params(a_re, a_im, log_dt, b_re, b_im, c_re, c_im):
    dt = jnp.exp(log_dt)[:, None]
    mag = jnp.exp(a_re * dt)
    lb_r, lb_i = mag * jnp.cos(a_im * dt), mag * jnp.sin(a_im * dt)
    den = a_re * a_re + a_im * a_im
    f_r = ((lb_r - 1.0) * a_re + lb_i * a_im) / den
    f_i = (lb_i * a_re - (lb_r - 1.0) * a_im) / den
    bb_r = f_r[..., None] * b_re - f_i[..., None] * b_im
    bb_i = f_r[..., None] * b_im + f_i[..., None] * b_re
    n_gb = S5_GROUPS // S5_GB
    eye = jnp.eye(S5_GB, dtype=F32)

    def blockdiag_in(bb):
        t = bb.reshape(n_gb, S5_GB, S5_STATE, S5_GROUP).transpose(0, 1, 3, 2)
        return jnp.einsum("bgip,gh->bgihp", t, eye).reshape(n_gb, S5_GB * S5_GROUP, S5_GB * S5_STATE)

    def blockdiag_out(c):
        t = c.reshape(n_gb, S5_GB, S5_GROUP, S5_STATE).transpose(0, 1, 3, 2)
        return jnp.einsum("bgpi,gh->bgphi", t, eye).reshape(n_gb, S5_GB * S5_STATE, S5_GB * S5_GROUP)

    wbb = jnp.concatenate([blockdiag_in(bb_r), blockdiag_in(bb_i)], axis=-1).astype(BF16)
    wc = jnp.concatenate([blockdiag_out(c_re), -blockdiag_out(c_im)], axis=1).astype(BF16)
    a_blk = jnp.stack([lb_r.reshape(n_gb, -1), lb_i.reshape(n_gb, -1)], axis=1)
    return wbb, wc, a_blk


def _s5_state_to_blocks(h_re, h_im):
    B = h_re.shape[0]
    n_gb = S5_GROUPS // S5_GB
    r = h_re.reshape(B, n_gb, -1).transpose(1, 0, 2)
    i = h_im.reshape(B, n_gb, -1).transpose(1, 0, 2)
    return jnp.concatenate([r, i], axis=-1)


def _s5_blocks_to_state(h_blk):
    n_gb, B, P2 = h_blk.shape
    r = h_blk[..., :P2 // 2].transpose(1, 0, 2).reshape(B, S5_GROUPS, S5_STATE)
    i = h_blk[..., P2 // 2:].transpose(1, 0, 2).reshape(B, S5_GROUPS, S5_STATE)
    return r, i


def _causal_kernel(x_ref, st_ref, p1_ref, p2_ref, o_ref, ctx, *, mode, S, H, Tb, pos0, ntime):
    j = pl.program_id(2)
    HS, TS = H * S, Tb * S

    @pl.when(j == 0)
    def _():
        ctx[0:HS, :] = st_ref[0]

    ctx[HS:HS + TS, :] = x_ref[...]

    def back(k, lanes):
        return ctx[HS - k * S:HS - k * S + TS, lanes]

    if mode == "conv":
        acc = p2_ref[...] + back(0, slice(None)) * p1_ref[SSD_CONV - 1:SSD_CONV, :]
        for k in range(1, SSD_CONV):
            acc = acc + back(k, slice(None)) * p1_ref[SSD_CONV - 1 - k:SSD_CONV - k, :]
        o_ref[...] = acc * jax.nn.sigmoid(acc)
    else:
        for gi, w in enumerate(POOL_WINDOWS):
            lanes = slice(gi * POOL_GROUP, (gi + 1) * POOL_GROUP)
            cur = back(0, lanes)
            win = cur
            for k in range(1, w):
                win = win + back(k, lanes)
            if pos0 + 1 >= w:
                mean = win * (1.0 / w)
            else:
                assert S == 1, "position-dependent window counts need step-contiguous rows"
                t = lax.broadcasted_iota(jnp.int32, (TS, POOL_GROUP), 0)
                cnt = jnp.minimum(pos0 + j * Tb + t + 1, w).astype(F32)
                mean = win / cnt
            mixed = _dot((mean - cur).astype(BF16), p1_ref[gi])
            o_ref[:, lanes] = mixed * p2_ref[:, lanes]

    if ntime > 1:
        ctx[0:HS, :] = ctx[TS:TS + HS, :]


def _causal(x, st, p1, p2, *, mode, nseq, S, H, Tb, L, pos0):
    C = x.shape[1]
    ncol = C // D_MODEL
    ntime = L // Tb
    assert L % Tb == 0 and (H * S) % SUBLANES == 0
    if mode == "conv":
        p1_spec = pl.BlockSpec((SSD_CONV, D_MODEL), lambda s, c, j: (0, c))
        p2_spec = pl.BlockSpec((1, D_MODEL), lambda s, c, j: (0, c))
    else:
        p1_spec = pl.BlockSpec(p1.shape, lambda s, c, j: (0, 0, 0))
        p2_spec = pl.BlockSpec((1, D_MODEL), lambda s, c, j: (0, 0))
    return pl.pallas_call(
        functools.partial(_causal_kernel, mode=mode, S=S, H=H, Tb=Tb, pos0=pos0, ntime=ntime),
        grid=(nseq, ncol, ntime),
        in_specs=[pl.BlockSpec((Tb * S, D_MODEL), lambda s, c, j: (s * ntime + j, c)),
                  pl.BlockSpec((1, H * S, D_MODEL), lambda s, c, j: (s, 0, c)),
                  p1_spec, p2_spec],
        out_specs=pl.BlockSpec((Tb * S, D_MODEL), lambda s, c, j: (s * ntime + j, c)),
        out_shape=jax.ShapeDtypeStruct(x.shape, F32),
        scratch_shapes=[pltpu.VMEM(((H + Tb) * S, D_MODEL), F32)],
        compiler_params=_cparams(("parallel", "parallel", "arbitrary"), VMEM_LIMIT_BYTES),
        name="causal_" + mode,
    )(x, st, p1, p2)


def _causal_group(x, hist, p1, p2, *, mode, B, L, pos0, n_hist):
    C = x.shape[1]
    if L % 256 == 0:
        H = POOL_HIST if mode == "pool" else SUBLANES
        st = jnp.pad(hist, ((0, 0), (H - n_hist, 0), (0, 0)))
        return _causal(x, st, p1, p2, mode=mode, nseq=B, S=1, H=H, Tb=256, L=L, pos0=pos0)
    H = POOL_HIST if mode == "pool" else n_hist
    x_tm = x.reshape(B, L, C).transpose(1, 0, 2).reshape(L * B, C)
    st = jnp.pad(hist, ((0, 0), (H - n_hist, 0), (0, 0))).transpose(1, 0, 2).reshape(1, H * B, C)
    y = _causal(x_tm, st, p1, p2, mode=mode, nseq=1, S=B, H=H, Tb=L, L=L, pos0=pos0)
    return y.reshape(L, B, C).transpose(1, 0, 2).reshape(B * L, C)


def _cmlp_kernel(z_ref, ws_ref, bias_ref, lg_ref, lb_ref, *out_refs, nsub, write_v):
    o_ref = out_refs[0]
    for c in range(nsub):
        rows = slice(c * CMLP_CHUNK, (c + 1) * CMLP_CHUNK)
        v = _ln(z_ref[rows, CMLP_WIDTH:2 * CMLP_WIDTH], lg_ref[...], lb_ref[...])
        if write_v:
            out_refs[1][rows, :] = v
        vb = v.astype(BF16)
        for h in range(CMLP_HEADS):
            lanes = slice(h * CMLP_HEAD_DIM, (h + 1) * CMLP_HEAD_DIM)
            mixed = _dot(ws_ref[h], vb[:, lanes]) + bias_ref[:, lanes]
            o_ref[rows, lanes] = z_ref[rows, lanes] * mixed


def _cmlp_core(z, ws_eff, bias_eff, ln_g, ln_b, write_v):
    M = z.shape[0]
    tm = min(512, M)
    assert M % tm == 0 and tm % CMLP_CHUNK == 0
    out_shape = [jax.ShapeDtypeStruct((M, CMLP_WIDTH), F32)]
    out_specs = [pl.BlockSpec((tm, CMLP_WIDTH), lambda i: (i, 0))]
    if write_v:
        out_shape.append(jax.ShapeDtypeStruct((M, CMLP_WIDTH), F32))
        out_specs.append(pl.BlockSpec((tm, CMLP_WIDTH), lambda i: (i, 0)))
    return pl.pallas_call(
        functools.partial(_cmlp_kernel, nsub=tm // CMLP_CHUNK, write_v=write_v),
        grid=(M // tm,),
        in_specs=[pl.BlockSpec((tm, 2 * CMLP_WIDTH), lambda i: (i, 0)),
                  pl.BlockSpec((CMLP_HEADS, CMLP_CHUNK, CMLP_CHUNK), lambda i: (0, 0, 0)),
                  pl.BlockSpec((CMLP_CHUNK, CMLP_WIDTH), lambda i: (0, 0)),
                  pl.BlockSpec((1, CMLP_WIDTH), lambda i: (0, 0)),
                  pl.BlockSpec((1, CMLP_WIDTH), lambda i: (0, 0))],
        out_specs=out_specs,
        out_shape=out_shape,
        compiler_params=_cparams(("parallel",), VMEM_LIMIT_BYTES),
        name="cmlp_core",
    )(z, ws_eff, bias_eff, ln_g.reshape(1, -1), ln_b.reshape(1, -1))


def _cmlp_mix_params(w_s, b_s, L):
    q = min(L, CMLP_CHUNK)
    nrep = CMLP_CHUNK // q
    causal = jnp.tril(jnp.ones((q, q), dtype=bool))
    ws = jnp.where(causal[None], w_s[:, :q, :q], 0.0)
    eye = jnp.eye(nrep, dtype=F32)
    ws_eff = jnp.einsum("ab,hts->hatbs", eye, ws).reshape(CMLP_HEADS, CMLP_CHUNK, CMLP_CHUNK)
    bias = jnp.tile(b_s[:, :q], (1, nrep))
    bias_eff = jnp.repeat(bias.T, CMLP_HEAD_DIM, axis=1)
    return ws_eff.astype(BF16), bias_eff


def _ssd_kernel(*refs, nsub, Lc, carry, chunks_per_seq):
    if carry:
        (z_ref, xa_ref, dtr_ref, tril_ref, rm_ref, dtb_ref, a_ref, ex_ref, ext_ref, dsk_ref, ng_ref,
         y_ref, ho_ref, h_scr, acs_scr, xdt_scr, c_scr, b_scr, y_scr) = refs
        h0_ref = None
    else:
        (z_ref, xa_ref, dtr_ref, tril_ref, rm_ref, dtb_ref, a_ref, ex_ref, ext_ref, dsk_ref, ng_ref, h0_ref,
         y_ref, ho_ref, h_scr, acs_scr, xdt_scr, c_scr, b_scr, y_scr) = refs
    ci = pl.program_id(0)
    sub = pl.program_id(1)
    R = SSD_CHUNK
    GW = SSD_HPG * SSD_HEAD_DIM

    if carry:
        @pl.when(ci % chunks_per_seq == 0)
        def _():
            h_scr[...] = jnp.zeros_like(h_scr)

    @pl.when(sub == 0)
    def _():
        trilf = tril_ref[...]
        dt = jax.nn.softplus(dtr_ref[...] + dtb_ref[...])
        a_cs = _dot_exact(trilf, dt * a_ref[...])
        acs_scr[...] = a_cs
        a_cs_t = a_cs.T
        xs = xa_ref[:, 0:SSD_INNER]
        xdt = xs * _dot_exact(dt, ex_ref[...])
        xdt_scr[...] = xdt
        y_scr[...] = dsk_ref[...] * xs
        for g in range(SSD_GROUPS):
            bm = xa_ref[:, SSD_INNER + g * SSD_STATE:SSD_INNER + (g + 1) * SSD_STATE]
            cm = xa_ref[:, SSD_INNER + SSD_GN + g * SSD_STATE:SSD_INNER + SSD_GN + (g + 1) * SSD_STATE]
            b_scr[:, g * SSD_STATE:(g + 1) * SSD_STATE] = bm
            cmb = cm.astype(BF16)
            c_scr[:, g * SSD_STATE:(g + 1) * SSD_STATE] = cmb
            cb = _dot_nt(cmb, bm.astype(BF16)) * trilf
            for jh in range(SSD_HPG):
                hd = g * SSD_HPG + jh
                lanes = slice(hd * SSD_HEAD_DIM, (hd + 1) * SSD_HEAD_DIM)
                seg = a_cs[:, hd:hd + 1] - a_cs_t[hd:hd + 1, :]
                lmat = (cb * jnp.exp(jnp.minimum(seg, 0.0))).astype(BF16)
                y_scr[:, lanes] += _dot(lmat, xdt[:, lanes].astype(BF16))

    rm = rm_ref[sub]
    a_cs = acs_scr[...]
    a_last = acs_scr[pl.ds(sub * Lc + Lc - 1, 1), :]
    e_in = _dot_exact(jnp.exp(a_cs) * rm, ex_ref[...])
    d_end = jnp.exp(jnp.minimum(a_last - a_cs, 0.0)) * rm
    xw = xdt_scr[...] * _dot_exact(d_end, ex_ref[...])
    cd = _dot_exact(ext_ref[...], jnp.broadcast_to(jnp.exp(a_last), (R, SSD_STATE)).T)
    for g in range(SSD_GROUPS):
        rows = slice(g * GW, (g + 1) * GW)
        if carry:
            hg = h_scr[rows, :]
        else:
            hg = h0_ref[0, rows, :]
        yoff = _dot_nt(c_scr[:, g * SSD_STATE:(g + 1) * SSD_STATE], hg.astype(BF16))
        y_scr[:, rows] += e_in[:, rows] * yoff
        st = _dot(xw[:, rows].T.astype(BF16), b_scr[:, g * SSD_STATE:(g + 1) * SSD_STATE].astype(BF16))
        hn = hg * cd[rows, :] + st
        if carry:
            h_scr[rows, :] = hn
        else:
            ho_ref[0, rows, :] = hn

    if carry:
        @pl.when(ci % chunks_per_seq == chunks_per_seq - 1)
        def _():
            ho_ref[0] = h_scr[...]

    @pl.when(sub == nsub - 1)
    def _():
        z = z_ref[...]
        yg = y_scr[...] * (z * jax.nn.sigmoid(z))
        for g in range(SSD_GROUPS):
            lanes = slice(g * GW, (g + 1) * GW)
            t = yg[:, lanes]
            r = lax.rsqrt(jnp.mean(t * t, axis=-1, keepdims=True) + RMS_EPS)
            y_ref[:, lanes] = t * r * ng_ref[:, lanes]


def _ssd_core(z, xa, dtr, h0, dt_bias, a_log, d_skip, norm_g, B, L):
    M = z.shape[0]
    R = SSD_CHUNK
    carry = L >= R
    if carry:
        assert L % R == 0 and h0 is None
        nsub, Lc, cps = 1, R, L // R
    else:
        assert R % L == 0 and M % R == 0
        nsub, Lc, cps = R // L, L, 1
    nchunk = M // R
    r = jnp.arange(R)
    same = (r[:, None] // Lc) == (r[None, :] // Lc)
    tril = (same & (r[None, :] <= r[:, None])).astype(F32)
    rm = ((r[None, :, None] // Lc) == jnp.arange(nsub)[:, None, None]).astype(F32)
    rm = jnp.broadcast_to(rm, (nsub, R, SSD_STATE))
    hsel = (jnp.arange(SSD_STATE)[:, None] == (jnp.arange(SSD_INNER)[None, :] // SSD_HEAD_DIM)).astype(F32)
    pad = SSD_STATE - SSD_HEADS
    dtb = jnp.pad(dt_bias, (0, pad)).reshape(1, SSD_STATE)
    a = jnp.pad(-jnp.exp(a_log), (0, pad)).reshape(1, SSD_STATE)
    dsk = jnp.repeat(d_skip, SSD_HEAD_DIM).reshape(1, SSD_INNER)
    HS = SSD_HEADS * SSD_HEAD_DIM
    const = lambda *shape: pl.BlockSpec(shape, lambda c, s: (0,) * len(shape))
    in_specs = [pl.BlockSpec((R, SSD_INNER), lambda c, s: (c, 0)),
                pl.BlockSpec((R, SSD_CONV_DIM), lambda c, s: (c, 0)),
                pl.BlockSpec((R, SSD_STATE), lambda c, s: (c, 0)),
                const(R, R), const(nsub, R, SSD_STATE), const(1, SSD_STATE), const(1, SSD_STATE),
                const(SSD_STATE, SSD_INNER), const(SSD_INNER, SSD_STATE), const(1, SSD_INNER),
                const(1, SSD_INNER)]
    args = [z, xa, dtr, tril, rm, dtb, a, hsel, hsel.T, dsk, norm_g.reshape(1, SSD_INNER)]
    if carry:
        ho_spec = pl.BlockSpec((1, HS, SSD_STATE), lambda c, s: (c // cps, 0, 0))
    else:
        in_specs.append(pl.BlockSpec((1, HS, SSD_STATE), lambda c, s: (c * nsub + s, 0, 0)))
        args.append(h0.reshape(B, HS, SSD_STATE))
        ho_spec = pl.BlockSpec((1, HS, SSD_STATE), lambda c, s: (c * nsub + s, 0, 0))
    return pl.pallas_call(
        functools.partial(_ssd_kernel, nsub=nsub, Lc=Lc, carry=carry, chunks_per_seq=cps),
        grid=(nchunk, nsub),
        in_specs=in_specs,
        out_specs=[pl.BlockSpec((R, SSD_INNER), lambda c, s: (c, 0)), ho_spec],
        out_shape=[jax.ShapeDtypeStruct((M, SSD_INNER), F32),
                   jax.ShapeDtypeStruct((B, HS, SSD_STATE), F32)],
        scratch_shapes=[pltpu.VMEM((HS, SSD_STATE), F32),
                        pltpu.VMEM((R, SSD_STATE), F32),
                        pltpu.VMEM((R, SSD_INNER), F32),
                        pltpu.VMEM((R, SSD_GN), BF16),
                        pltpu.VMEM((R, SSD_GN), F32),
                        pltpu.VMEM((R, SSD_INNER), F32)],
        compiler_params=_cparams(("arbitrary", "arbitrary"), VMEM_LIMIT_BYTES),
        name="ssd_core",
    )(*args)


def _peer_route_kernel(x_ref, wq_ref, keys_ref, s2_ref, e2_ref, thr_ref, c1_ref,
                       s_scr, m_scr, c_scr):
    Tq = x_ref.shape[0]
    NEG = -jnp.inf
    q = _dot(x_ref[...].astype(BF16), wq_ref[...]).astype(BF16)
    for h in range(PEER_HEADS):
        for half in range(2):
            col = (h * 2 + half) * PEER_HALF
            s = _dot_nt(keys_ref[h, half], q[:, col:col + PEER_HALF])
            s_scr[half, h] = s
            rem = s
            for r in range(PEER_ROUNDS):
                m = jnp.max(rem, axis=0, keepdims=True)
                eq = rem == m
                m_scr[half, r, h:h + 1, :] = m
                c_scr[half, r, h:h + 1, :] = jnp.sum(eq.astype(F32), axis=0, keepdims=True)
                rem = jnp.where(eq, NEG, rem)

    pairs = [(r, c) for r in range(PEER_ROUNDS) for c in range(PEER_ROUNDS)
             if (r + 1) * (c + 1) <= PEER_ROUNDS]
    sums = [m_scr[0, r] + m_scr[1, c] for r, c in pairs]
    wts = [c_scr[0, r] * c_scr[1, c] for r, c in pairs]
    cum = jnp.zeros((PEER_HEADS, Tq), F32)
    v16 = jnp.full((PEER_HEADS, Tq), NEG, F32)
    v17 = jnp.full((PEER_HEADS, Tq), NEG, F32)
    for _ in range(PEER_ROUNDS):
        mx = functools.reduce(jnp.maximum, sums)
        hit = [sv == mx for sv in sums]
        new = cum + functools.reduce(lambda p, t: p + t, [jnp.where(hm, w, 0.0) for hm, w in zip(hit, wts)])
        v16 = jnp.where((cum < PEER_TOPK) & (new >= PEER_TOPK), mx, v16)
        v17 = jnp.where((cum < PEER_TOPK + 1) & (new >= PEER_TOPK + 1), mx, v17)
        sums = [jnp.where(hm, NEG, sv) for hm, sv in zip(hit, sums)]
        cum = new
    tau = 0.5 * (v16 + v17)
    m1, m2 = m_scr[0, 0], m_scr[1, 0]
    zsum = jnp.zeros((PEER_HEADS, Tq), F32)
    for r, c in pairs:
        a1, a2 = m_scr[0, r], m_scr[1, c]
        val = c_scr[0, r] * c_scr[1, c] * jnp.exp(a1 - m1) * jnp.exp(a2 - m2)
        zsum = zsum + jnp.where(a2 >= tau - a1, val, 0.0)
    zinv = 1.0 / zsum
    for h in range(PEER_HEADS):
        s1, s2 = s_scr[0, h], s_scr[1, h]
        thr_ref[h] = tau[h:h + 1, :] - s1
        c1_ref[h] = jnp.exp(s1 - m1[h:h + 1, :]) * zinv[h:h + 1, :]
        s2_ref[h] = s2
        e2_ref[h] = jnp.exp(s2 - m2[h:h + 1, :])


def _peer_route(hn, wq, keys):
    T = hn.shape[0]
    Tq = min(PEER_TQ, T)
    assert T % Tq == 0
    out = jax.ShapeDtypeStruct((PEER_HEADS, PEER_NKEYS, T), F32)
    ospec = pl.BlockSpec((PEER_HEADS, PEER_NKEYS, Tq), lambda i: (0, 0, i))
    return pl.pallas_call(
        _peer_route_kernel,
        grid=(T // Tq,),
        in_specs=[pl.BlockSpec((Tq, D_MODEL), lambda i: (i, 0)),
                  pl.BlockSpec(wq.shape, lambda i: (0, 0)),
                  pl.BlockSpec(keys.shape, lambda i: (0, 0, 0, 0))],
        out_specs=[ospec] * 4,
        out_shape=[out] * 4,
        scratch_shapes=[pltpu.VMEM((2, PEER_HEADS, PEER_NKEYS, Tq), F32),
                        pltpu.VMEM((2, PEER_ROUNDS, PEER_HEADS, Tq), F32),
                        pltpu.VMEM((2, PEER_ROUNDS, PEER_HEADS, Tq), F32)],
        compiler_params=_cparams(("parallel",), VMEM_LIMIT_BYTES),
        name="peer_route",
    )(hn, wq, keys)


def _peer_ffn_kernel(x_ref, u_ref, vt_ref, s2_ref, e2_ref, thr_ref, c1_ref, g_ref, b_ref,
                     o_ref, xb_scr, s_scr, p_scr, acc_scr):
    c = pl.program_id(1)
    Tb = x_ref.shape[0]
    ni = PEER_EC // PEER_NKEYS
    assert ni == SUBLANES

    @pl.when(c == 0)
    def _():
        xb_scr[...] = x_ref[...].astype(BF16)
        acc_scr[...] = jnp.zeros_like(acc_scr)

    s_scr[...] = _dot_nt(u_ref[...], xb_scr[...])

    irows = pl.ds(pl.multiple_of(c * ni, ni), ni)
    for tt in range(Tb // LANES):
        lanes = slice(tt * LANES, (tt + 1) * LANES)
        thr = [thr_ref[h, irows, lanes] for h in range(PEER_HEADS)]
        c1 = [c1_ref[h, irows, lanes] for h in range(PEER_HEADS)]
        for ii in range(ni):
            rows = slice(ii * PEER_NKEYS, (ii + 1) * PEER_NKEYS)
            gate = jnp.zeros((PEER_NKEYS, LANES), F32)
            for h in range(PEER_HEADS):
                sel = s2_ref[h, :, lanes] >= thr[h][ii:ii + 1, :]
                gate = gate + jnp.where(sel, e2_ref[h, :, lanes] * c1[h][ii:ii + 1, :], 0.0)
            p_scr[rows, lanes] = (jax.nn.gelu(s_scr[rows, lanes]) * gate).astype(BF16)
    acc_scr[...] += _dot(vt_ref[...], p_scr[...])

    @pl.when(c == pl.num_programs(1) - 1)
    def _():
        o_ref[...] = _ln(ALPHA * x_ref[...] + acc_scr[...].T, g_ref[...], b_ref[...])


def _peer_ffn(hn, u, vt, s2, e2, thr, c1, g, b):
    T = hn.shape[0]
    E = u.shape[0]
    Tb = min(PEER_TB, T)
    assert T % Tb == 0 and E % PEER_EC == 0
    rspec = pl.BlockSpec((PEER_HEADS, PEER_NKEYS, Tb), lambda i, c: (0, 0, i))
    return pl.pallas_call(
        _peer_ffn_kernel,
        grid=(T // Tb, E // PEER_EC),
        in_specs=[pl.BlockSpec((Tb, D_MODEL), lambda i, c: (i, 0)),
                  pl.BlockSpec((PEER_EC, D_MODEL), lambda i, c: (c, 0)),
                  pl.BlockSpec((D_MODEL, PEER_EC), lambda i, c: (0, c)),
                  rspec, rspec, rspec, rspec,
                  pl.BlockSpec((1, D_MODEL), lambda i, c: (0, 0)),
                  pl.BlockSpec((1, D_MODEL), lambda i, c: (0, 0))],
        out_specs=pl.BlockSpec((Tb, D_MODEL), lambda i, c: (i, 0)),
        out_shape=jax.ShapeDtypeStruct((T, D_MODEL), F32),
        scratch_shapes=[pltpu.VMEM((Tb, D_MODEL), BF16),
                        pltpu.VMEM((PEER_EC, Tb), F32),
                        pltpu.VMEM((PEER_EC, Tb), BF16),
                        pltpu.VMEM((D_MODEL, Tb), F32)],
        compiler_params=_cparams(("parallel", "arbitrary"), VMEM_LIMIT_BYTES),
        name="peer_ffn",
    )(hn, u, vt, s2, e2, thr, c1, g.reshape(1, -1), b.reshape(1, -1))


def _trunk(x, pos0, s5_re, s5_im, pool_buf, conv_buf, ssd_h, W):
    B, L, _ = x.shape
    T = B * L
    h = x.reshape(T, D_MODEL)
    to_tm = lambda a: a.reshape(B, L, -1).transpose(1, 0, 2).reshape(T, -1)
    to_bm = lambda a: a.reshape(L, B, -1).transpose(1, 0, 2).reshape(T, -1)
    cmlp_v = None
    for i in range(DEPTH):
        kind = i % 4
        if kind == 0:
            u = _mm(h, W["s5_w_in"])
            g_tm, hl = _s5_core(to_tm(u), _s5_state_to_blocks(s5_re, s5_im), W["s5_wbb"], W["s5_wc"],
                                W["s5_a"], W["s5_d"], B, L)
            s5_re, s5_im = _s5_blocks_to_state(hl)
            g = to_bm(g_tm)
            mid = _mm(g, W["s5_w_glu"], bias=W["s5_b_glu"], gate=g)
            w_out = W["s5_w_out"]
        elif kind == 1:
            u = _mm(h, W["pool_w_in"])
            mid = _causal_group(u, pool_buf, W["pool_w_grp"], W["pool_scale"].reshape(1, -1),
                                mode="pool", B=B, L=L, pos0=pos0, n_hist=POOL_BUF)
            pool_buf = jnp.concatenate([pool_buf, u.reshape(B, L, -1)], axis=1)[:, -POOL_BUF:]
            w_out = W["pool_w_out"]
        elif kind == 2:
            z = _mm(h, W["cmlp_w_in"], bias=W["cmlp_b_in"], act="gelu")
            ws_eff, bias_eff = _cmlp_mix_params(W["cmlp_w_s"], W["cmlp_b_s"], L)
            outs = _cmlp_core(z, ws_eff, bias_eff, W["cmlp_ln_g"], W["cmlp_ln_b"], write_v=True)
            mid, cmlp_v = outs[0], outs[1].reshape(B, L, CMLP_WIDTH)
            w_out = W["cmlp_w_out"]
        else:
            zg = _mm(h, W["ssd_w_z"])
            xbc = _mm(h, W["ssd_w_xbc"])
            dtr = _mm(h, W["ssd_w_dt"])
            xa = _causal_group(xbc, conv_buf, W["ssd_conv_w"], W["ssd_conv_b"].reshape(1, -1),
                               mode="conv", B=B, L=L, pos0=pos0, n_hist=SSD_CONV - 1)
            conv_buf = jnp.concatenate([conv_buf, xbc.reshape(B, L, -1)], axis=1)[:, -(SSD_CONV - 1):]
            mid, hs = _ssd_core(zg, xa, dtr, ssd_h, W["ssd_dt_bias"], W["ssd_a_log"], W["ssd_d"],
                                W["ssd_norm_g"], B, L)
            ssd_h = hs.reshape(B, SSD_HEADS, SSD_HEAD_DIM, SSD_STATE)
            w_out = W["ssd_w_out"]
        h = _mm_ln(mid, w_out, h, W["ln1_g"][i], W["ln1_b"][i])
        s2, e2, thr, c1 = _peer_route(h, W["peer_w_q"][i], W["peer_keys"][i])
        h = _peer_ffn(h, W["peer_u"][i], W["peer_vt"][i], s2, e2, thr, c1, W["ln2_g"][i], W["ln2_b"][i])
    return h.reshape(B, L, D_MODEL), s5_re, s5_im, pool_buf, cmlp_v, conv_buf, ssd_h


def kernel(x_prompt, x_sample, state_s5_re, state_s5_im, state_pool, state_ssd_conv, state_ssd, s5_w_in, s5_a_re, s5_a_im, s5_log_dt, s5_b_re, s5_b_im, s5_c_re, s5_c_im, s5_d, s5_w_glu, s5_b_glu, s5_w_out, pool_w_in, pool_w_grp, pool_scale, pool_w_out, cmlp_w_in, cmlp_b_in, cmlp_ln_g, cmlp_ln_b, cmlp_w_s, cmlp_b_s, cmlp_w_out, ssd_w_in, ssd_conv_w, ssd_conv_b, ssd_dt_bias, ssd_a_log, ssd_d, ssd_norm_g, ssd_w_out, ln1_g, ln1_b, ln2_g, ln2_b, peer_w_q, peer_keys, peer_u, peer_v):
    bf = lambda a: a.astype(BF16)
    wbb, wc, a_blk = _s5_params(s5_a_re, s5_a_im, s5_log_dt, s5_b_re, s5_b_im, s5_c_re, s5_c_im)
    dt_cols = SSD_STATE - SSD_HEADS
    W = dict(
        s5_w_in=bf(s5_w_in), s5_wbb=wbb, s5_wc=wc, s5_a=a_blk, s5_d=s5_d,
        s5_w_glu=bf(s5_w_glu), s5_b_glu=s5_b_glu, s5_w_out=bf(s5_w_out),
        pool_w_in=bf(pool_w_in), pool_w_grp=bf(pool_w_grp), pool_scale=pool_scale, pool_w_out=bf(pool_w_out),
        cmlp_w_in=bf(cmlp_w_in), cmlp_b_in=cmlp_b_in, cmlp_ln_g=cmlp_ln_g, cmlp_ln_b=cmlp_ln_b,
        cmlp_w_s=cmlp_w_s, cmlp_b_s=cmlp_b_s, cmlp_w_out=bf(cmlp_w_out),
        ssd_w_z=bf(ssd_w_in[:, :SSD_INNER]),
        ssd_w_xbc=bf(ssd_w_in[:, SSD_INNER:SSD_INNER + SSD_CONV_DIM]),
        ssd_w_dt=bf(jnp.pad(ssd_w_in[:, SSD_INNER + SSD_CONV_DIM:], ((0, 0), (0, dt_cols)))),
        ssd_conv_w=ssd_conv_w, ssd_conv_b=ssd_conv_b, ssd_dt_bias=ssd_dt_bias, ssd_a_log=ssd_a_log,
        ssd_d=ssd_d, ssd_norm_g=ssd_norm_g, ssd_w_out=bf(ssd_w_out),
        ln1_g=ln1_g, ln1_b=ln1_b, ln2_g=ln2_g, ln2_b=ln2_b,
        peer_w_q=bf(peer_w_q), peer_keys=bf(peer_keys), peer_u=bf(peer_u),
        peer_vt=bf(peer_v).transpose(0, 2, 1),
    )
    bp = x_prompt.shape[0]
    zeros = lambda *s: jnp.zeros(s, F32)
    (y_p, s5_re_p, s5_im_p, pool_p, _, conv_p, ssd_p) = _trunk(
        x_prompt, 0, zeros(bp, S5_GROUPS, S5_STATE), zeros(bp, S5_GROUPS, S5_STATE),
        zeros(bp, POOL_BUF, D_MODEL), zeros(bp, SSD_CONV - 1, SSD_CONV_DIM), None, W)
    past_len = 16384
    (y_s, s5_re_s, s5_im_s, pool_s, cmlp_v_s, conv_s, ssd_s) = _trunk(
        x_sample, past_len, state_s5_re, state_s5_im, state_pool, state_ssd_conv, state_ssd, W)
    return (y_p, y_s, s5_re_p, s5_im_p, pool_p, conv_p, ssd_p,
            s5_re_s, s5_im_s, pool_s, cmlp_v_s, conv_s, ssd_s)
```

```python
import functools
import math

import jax
import jax.numpy as jnp
from jax import lax
from jax.experimental import pallas as pl
from jax.experimental.pallas import tpu as pltpu

F32 = jnp.float32
BF16 = jnp.bfloat16

D_MODEL = 1024
DEPTH = 4
ALPHA = (2 * DEPTH) ** 0.25
LN_EPS = 1e-5
RMS_EPS = 1e-5

S5_GROUPS = 64
S5_GROUP = 16
S5_STATE = 64
S5_GB = 8
S5_ROWS = 1024

POOL_WINDOWS = (2, 4, 8, 16)
POOL_GROUP = 256
POOL_BUF = 15
POOL_HIST = 16

CMLP_WIDTH = 1024
CMLP_CHUNK = 128
CMLP_HEADS = 4
CMLP_HEAD_DIM = 256

SSD_INNER = 2048
SSD_HEAD_DIM = 64
SSD_HEADS = 32
SSD_STATE = 128
SSD_GROUPS = 4
SSD_HPG = SSD_HEADS // SSD_GROUPS
SSD_CONV = 4
SSD_CHUNK = 128
SSD_CONV_DIM = SSD_INNER + 2 * SSD_GROUPS * SSD_STATE
SSD_GN = SSD_GROUPS * SSD_STATE

PEER_HEADS = 8
PEER_NKEYS = 128
PEER_TOPK = 16
PEER_HALF = 128
PEER_ROUNDS = PEER_TOPK + 1
PEER_TQ = 256
PEER_TB = 512
PEER_EC = 1024

GELU_K0 = -2.0 * math.sqrt(2.0 / math.pi) * math.log2(math.e)
GELU_K1 = 0.044715 * GELU_K0

LANES = 128
SUBLANES = 8
VMEM_LIMIT_BYTES = 56 * 1024 * 1024


def _cparams(semantics, vmem=None):
    return pltpu.CompilerParams(dimension_semantics=semantics, vmem_limit_bytes=vmem)


def _ln(y, g, b):
    mu = jnp.mean(y, axis=-1, keepdims=True)
    yc = y - mu
    var = jnp.mean(yc * yc, axis=-1, keepdims=True)
    return yc * lax.rsqrt(var + LN_EPS) * g + b


def _gelu(x):
    w = x * (GELU_K0 + GELU_K1 * (x * x))
    return x / (1.0 + jnp.exp2(w))


def _dot(a, b):
    return jnp.dot(a, b, preferred_element_type=F32)


def _dot_nt(a, b):
    return lax.dot_general(a, b, (((1,), (1,)), ((), ())), preferred_element_type=F32)


def _dot_exact(a, b):
    return jnp.dot(a, b, preferred_element_type=F32, precision=lax.Precision.HIGHEST)


def _mm_kernel(*refs, act, has_bias, has_gate):
    x_ref, w_ref = refs[0], refs[1]
    k = 2
    acc = _dot(x_ref[...].astype(BF16), w_ref[...])
    if has_bias:
        acc = acc + refs[k][...]
        k += 1
    if act == "gelu":
        acc = _gelu(acc)
    if has_gate:
        acc = refs[k][...] * jax.nn.sigmoid(acc)
        k += 1
    refs[k][...] = acc


def _mm(x, w, bias=None, act=None, gate=None, tm=512, tn=512):
    M, K = x.shape
    N = w.shape[1]
    tm, tn = min(tm, M), min(tn, N)
    assert M % tm == 0 and N % tn == 0
    in_specs = [pl.BlockSpec((tm, K), lambda i, j: (i, 0)),
                pl.BlockSpec((K, tn), lambda i, j: (0, j))]
    args = [x, w]
    if bias is not None:
        in_specs.append(pl.BlockSpec((1, tn), lambda i, j: (0, j)))
        args.append(bias.reshape(1, N))
    if gate is not None:
        in_specs.append(pl.BlockSpec((tm, tn), lambda i, j: (i, j)))
        args.append(gate)
    return pl.pallas_call(
        functools.partial(_mm_kernel, act=act, has_bias=bias is not None, has_gate=gate is not None),
        grid=(M // tm, N // tn),
        in_specs=in_specs,
        out_specs=pl.BlockSpec((tm, tn), lambda i, j: (i, j)),
        out_shape=jax.ShapeDtypeStruct((M, N), F32),
        compiler_params=_cparams(("parallel", "parallel"), VMEM_LIMIT_BYTES),
        name="mm",
    )(*args)


def _mm_ln_kernel(x_ref, w_ref, h_ref, g_ref, b_ref, o_ref):
    acc = _dot(x_ref[...].astype(BF16), w_ref[...])
    o_ref[...] = _ln(ALPHA * h_ref[...] + acc, g_ref[...], b_ref[...])


def _mm_ln(x, w, h, g, b, tm=256):
    M, K = x.shape
    N = w.shape[1]
    tm = min(tm, M)
    assert M % tm == 0
    return pl.pallas_call(
        _mm_ln_kernel,
        grid=(M // tm,),
        in_specs=[pl.BlockSpec((tm, K), lambda i: (i, 0)),
                  pl.BlockSpec((K, N), lambda i: (0, 0)),
                  pl.BlockSpec((tm, N), lambda i: (i, 0)),
                  pl.BlockSpec((1, N), lambda i: (0, 0)),
                  pl.BlockSpec((1, N), lambda i: (0, 0))],
        out_specs=pl.BlockSpec((tm, N), lambda i: (i, 0)),
        out_shape=jax.ShapeDtypeStruct((M, N), F32),
        compiler_params=_cparams(("parallel",), VMEM_LIMIT_BYTES),
        name="mm_ln",
    )(x, w, h, g.reshape(1, N), b.reshape(1, N))


def _s5_kernel(u_ref, wbb_ref, wc_ref, a_ref, d_ref, h0_ref, g_ref, hl_ref,
               h_scr, bu_scr, hs_scr, *, B, Tc):
    tc = pl.program_id(1)
    P = S5_GB * S5_STATE

    @pl.when(tc == 0)
    def _():
        h_scr[...] = h0_ref[0]

    u = u_ref[...]
    bu_scr[...] = _dot(u.astype(BF16), wbb_ref[0])
    ar = jnp.broadcast_to(a_ref[0, 0:1, :], (SUBLANES, P))
    ai = jnp.broadcast_to(a_ref[0, 1:2, :], (SUBLANES, P))

    def seq_body(sb, carry):
        r0 = pl.multiple_of(sb * SUBLANES, SUBLANES)

        def t_body(t, h):
            hr, hi = h
            rows = pl.ds(pl.multiple_of(t * B + r0, SUBLANES), SUBLANES)
            nr = ar * hr - ai * hi + bu_scr[rows, 0:P]
            ni = ar * hi + ai * hr + bu_scr[rows, P:2 * P]
            hs_scr[rows, 0:P] = nr
            hs_scr[rows, P:2 * P] = ni
            return nr, ni

        hr, hi = lax.fori_loop(0, Tc, t_body,
                               (h_scr[pl.ds(r0, SUBLANES), 0:P], h_scr[pl.ds(r0, SUBLANES), P:2 * P]))
        h_scr[pl.ds(r0, SUBLANES), 0:P] = hr
        h_scr[pl.ds(r0, SUBLANES), P:2 * P] = hi
        return carry

    lax.fori_loop(0, B // SUBLANES, seq_body, 0)
    y = _dot(hs_scr[...].astype(BF16), wc_ref[0]) + d_ref[...] * u
    g_ref[...] = _gelu(y)

    @pl.when(tc == pl.num_programs(1) - 1)
    def _():
        hl_ref[0] = h_scr[...]


def _s5_core(u_tm, h0_blk, wbb, wc, a_blk, d_skip, B, L):
    Tc = S5_ROWS // B
    assert B % SUBLANES == 0 and L % Tc == 0
    P2 = 2 * S5_GB * S5_STATE
    W = S5_GB * S5_GROUP
    n_gb = S5_GROUPS // S5_GB
    return pl.pallas_call(
        functools.partial(_s5_kernel, B=B, Tc=Tc),
        grid=(n_gb, L // Tc),
        in_specs=[pl.BlockSpec((S5_ROWS, W), lambda g, t: (t, g)),
                  pl.BlockSpec((1, W, P2), lambda g, t: (g, 0, 0)),
                  pl.BlockSpec((1, P2, W), lambda g, t: (g, 0, 0)),
                  pl.BlockSpec((1, 2, P2 // 2), lambda g, t: (g, 0, 0)),
                  pl.BlockSpec((1, W), lambda g, t: (0, g)),
                  pl.BlockSpec((1, B, P2), lambda g, t: (g, 0, 0))],
        out_specs=[pl.BlockSpec((S5_ROWS, W), lambda g, t: (t, g)),
                   pl.BlockSpec((1, B, P2), lambda g, t: (g, 0, 0))],
        out_shape=[jax.ShapeDtypeStruct((L * B, D_MODEL), F32),
                   jax.ShapeDtypeStruct((n_gb, B, P2), F32)],
        scratch_shapes=[pltpu.VMEM((B, P2), F32),
                        pltpu.VMEM((S5_ROWS, P2), F32),
                        pltpu.VMEM((S5_ROWS, P2), F32)],
        compiler_params=_cparams(("parallel", "arbitrary"), VMEM_LIMIT_BYTES),
        name="s5_core",
    )(u_tm, wbb, wc, a_blk, d_skip.reshape(1, D_MODEL), h0_blk)


def _s5_params(a_re, a_im, log_dt, b_re, b_im, c_re, c_im):
    dt = jnp.exp(log_dt)[:, None]
    mag = jnp.exp(a_re * dt)
    lb_r, lb_i = mag * jnp.cos(a_im * dt), mag * jnp.sin(a_im * dt)
    den = a_re * a_re + a_im * a_im
    f_r = ((lb_r - 1.0) * a_re + lb_i * a_im) / den
    f_i = (lb_i * a_re - (lb_r - 1.0) * a_im) / den
    bb_r = f_r[..., None] * b_re - f_i[..., None] * b_im
    bb_i = f_r[..., None] * b_im + f_i[..., None] * b_re
    n_gb = S5_GROUPS // S5_GB
    eye = jnp.eye(S5_GB, dtype=F32)

    def blockdiag_in(bb):
        t = bb.reshape(n_gb, S5_GB, S5_STATE, S5_GROUP).transpose(0, 1, 3, 2)
        return jnp.einsum("bgip,gh->bgihp", t, eye).reshape(n_gb, S5_GB * S5_GROUP, S5_GB * S5_STATE)

    def blockdiag_out(c):
        t = c.reshape(n_gb, S5_GB, S5_GROUP, S5_STATE).transpose(0, 1, 3, 2)
        return jnp.einsum("bgpi,gh->bgphi", t, eye).reshape(n_gb, S5_GB * S5_STATE, S5_GB * S5_GROUP)

    wbb = jnp.concatenate([blockdiag_in(bb_r), blockdiag_in(bb_i)], axis=-1).astype(BF16)
    wc = jnp.concatenate([blockdiag_out(c_re), -blockdiag_out(c_im)], axis=1).astype(BF16)
    a_blk = jnp.stack([lb_r.reshape(n_gb, -1), lb_i.reshape(n_gb, -1)], axis=1)
    return wbb, wc, a_blk


def _s5_state_to_blocks(h_re, h_im):
    B = h_re.shape[0]
    n_gb = S5_GROUPS // S5_GB
    r = h_re.reshape(B, n_gb, -1).transpose(1, 0, 2)
    i = h_im.reshape(B, n_gb, -1).transpose(1, 0, 2)
    return jnp.concatenate([r, i], axis=-1)


def _s5_blocks_to_state(h_blk):
    n_gb, B, P2 = h_blk.shape
    r = h_blk[..., :P2 // 2].transpose(1, 0, 2).reshape(B, S5_GROUPS, S5_STATE)
    i = h_blk[..., P2 // 2:].transpose(1, 0, 2).reshape(B, S5_GROUPS, S5_STATE)
    return r, i


def _causal_kernel(x_ref, st_ref, p1_ref, p2_ref, o_ref, ctx, *, mode, S, H, Tb, pos0, ntime):
    j = pl.program_id(2)
    HS, TS = H * S, Tb * S

    @pl.when(j == 0)
    def _():
        ctx[0:HS, :] = st_ref[0]

    ctx[HS:HS + TS, :] = x_ref[...]

    def back(k, lanes):
        return ctx[HS - k * S:HS - k * S + TS, lanes]

    if mode == "conv":
        acc = p2_ref[...] + back(0, slice(None)) * p1_ref[SSD_CONV - 1:SSD_CONV, :]
        for k in range(1, SSD_CONV):
            acc = acc + back(k, slice(None)) * p1_ref[SSD_CONV - 1 - k:SSD_CONV - k, :]
        o_ref[...] = acc * jax.nn.sigmoid(acc)
    else:
        for gi, w in enumerate(POOL_WINDOWS):
            lanes = slice(gi * POOL_GROUP, (gi + 1) * POOL_GROUP)
            cur = back(0, lanes)
            win = cur
            for k in range(1, w):
                win = win + back(k, lanes)
            if pos0 + 1 >= w:
                mean = win * (1.0 / w)
            else:
                assert S == 1, "position-dependent window counts need step-contiguous rows"
                t = lax.broadcasted_iota(jnp.int32, (TS, POOL_GROUP), 0)
                cnt = jnp.minimum(pos0 + j * Tb + t + 1, w).astype(F32)
                mean = win / cnt
            mixed = _dot((mean - cur).astype(BF16), p1_ref[gi])
            o_ref[:, lanes] = mixed * p2_ref[:, lanes]

    if ntime > 1:
        ctx[0:HS, :] = ctx[TS:TS + HS, :]


def _causal(x, st, p1, p2, *, mode, nseq, S, H, Tb, L, pos0):
    C = x.shape[1]
    ncol = C // D_MODEL
    ntime = L // Tb
    assert L % Tb == 0 and (H * S) % SUBLANES == 0
    if mode == "conv":
        p1_spec = pl.BlockSpec((SSD_CONV, D_MODEL), lambda s, c, j: (0, c))
        p2_spec = pl.BlockSpec((1, D_MODEL), lambda s, c, j: (0, c))
    else:
        p1_spec = pl.BlockSpec(p1.shape, lambda s, c, j: (0, 0, 0))
        p2_spec = pl.BlockSpec((1, D_MODEL), lambda s, c, j: (0, 0))
    return pl.pallas_call(
        functools.partial(_causal_kernel, mode=mode, S=S, H=H, Tb=Tb, pos0=pos0, ntime=ntime),
        grid=(nseq, ncol, ntime),
        in_specs=[pl.BlockSpec((Tb * S, D_MODEL), lambda s, c, j: (s * ntime + j, c)),
                  pl.BlockSpec((1, H * S, D_MODEL), lambda s, c, j: (s, 0, c)),
                  p1_spec, p2_spec],
        out_specs=pl.BlockSpec((Tb * S, D_MODEL), lambda s, c, j: (s * ntime + j, c)),
        out_shape=jax.ShapeDtypeStruct(x.shape, F32),
        scratch_shapes=[pltpu.VMEM(((H + Tb) * S, D_MODEL), F32)],
        compiler_params=_cparams(("parallel", "parallel", "arbitrary"), VMEM_LIMIT_BYTES),
        name="causal_" + mode,
    )(x, st, p1, p2)


def _causal_group(x, hist, p1, p2, *, mode, B, L, pos0, n_hist):
    C = x.shape[1]
    if L % 256 == 0:
        H = POOL_HIST if mode == "pool" else SUBLANES
        st = jnp.pad(hist, ((0, 0), (H - n_hist, 0), (0, 0)))
        return _causal(x, st, p1, p2, mode=mode, nseq=B, S=1, H=H, Tb=256, L=L, pos0=pos0)
    H = POOL_HIST if mode == "pool" else n_hist
    x_tm = x.reshape(B, L, C).transpose(1, 0, 2).reshape(L * B, C)
    st = jnp.pad(hist, ((0, 0), (H - n_hist, 0), (0, 0))).transpose(1, 0, 2).reshape(1, H * B, C)
    y = _causal(x_tm, st, p1, p2, mode=mode, nseq=1, S=B, H=H, Tb=L, L=L, pos0=pos0)
    return y.reshape(L, B, C).transpose(1, 0, 2).reshape(B * L, C)


def _cmlp_kernel(z_ref, ws_ref, bias_ref, lg_ref, lb_ref, *out_refs, nsub, write_v):
    o_ref = out_refs[0]
    for c in range(nsub):
        rows = slice(c * CMLP_CHUNK, (c + 1) * CMLP_CHUNK)
        v = _ln(z_ref[rows, CMLP_WIDTH:2 * CMLP_WIDTH], lg_ref[...], lb_ref[...])
        if write_v:
            out_refs[1][rows, :] = v
        vb = v.astype(BF16)
        for h in range(CMLP_HEADS):
            lanes = slice(h * CMLP_HEAD_DIM, (h + 1) * CMLP_HEAD_DIM)
            mixed = _dot(ws_ref[h], vb[:, lanes]) + bias_ref[:, lanes]
            o_ref[rows, lanes] = z_ref[rows, lanes] * mixed


def _cmlp_core(z, ws_eff, bias_eff, ln_g, ln_b, write_v):
    M = z.shape[0]
    tm = min(512, M)
    assert M % tm == 0 and tm % CMLP_CHUNK == 0
    out_shape = [jax.ShapeDtypeStruct((M, CMLP_WIDTH), F32)]
    out_specs = [pl.BlockSpec((tm, CMLP_WIDTH), lambda i: (i, 0))]
    if write_v:
        out_shape.append(jax.ShapeDtypeStruct((M, CMLP_WIDTH), F32))
        out_specs.append(pl.BlockSpec((tm, CMLP_WIDTH), lambda i: (i, 0)))
    return pl.pallas_call(
        functools.partial(_cmlp_kernel, nsub=tm // CMLP_CHUNK, write_v=write_v),
        grid=(M // tm,),
        in_specs=[pl.BlockSpec((tm, 2 * CMLP_WIDTH), lambda i: (i, 0)),
                  pl.BlockSpec((CMLP_HEADS, CMLP_CHUNK, CMLP_CHUNK), lambda i: (0, 0, 0)),
                  pl.BlockSpec((CMLP_CHUNK, CMLP_WIDTH), lambda i: (0, 0)),
                  pl.BlockSpec((1, CMLP_WIDTH), lambda i: (0, 0)),
                  pl.BlockSpec((1, CMLP_WIDTH), lambda i: (0, 0))],
        out_specs=out_specs,
        out_shape=out_shape,
        compiler_params=_cparams(("parallel",), VMEM_LIMIT_BYTES),
        name="cmlp_core",
    )(z, ws_eff, bias_eff, ln_g.reshape(1, -1), ln_b.reshape(1, -1))


def _cmlp_mix_params(w_s, b_s, L):
    q = min(L, CMLP_CHUNK)
    nrep = CMLP_CHUNK // q
    causal = jnp.tril(jnp.ones((q, q), dtype=bool))
    ws = jnp.where(causal[None], w_s[:, :q, :q], 0.0)
    eye = jnp.eye(nrep, dtype=F32)
    ws_eff = jnp.einsum("ab,hts->hatbs", eye, ws).reshape(CMLP_HEADS, CMLP_CHUNK, CMLP_CHUNK)
    bias = jnp.tile(b_s[:, :q], (1, nrep))
    bias_eff = jnp.repeat(bias.T, CMLP_HEAD_DIM, axis=1)
    return ws_eff.astype(BF16), bias_eff


def _ssd_kernel(*refs, nsub, Lc, carry, chunks_per_seq):
    if carry:
        (z_ref, xa_ref, dtr_ref, tril_ref, rm_ref, dtb_ref, a_ref, ex_ref, ext_ref, dsk_ref, ng_ref,
         y_ref, ho_ref, h_scr, acs_scr, xdt_scr, c_scr, b_scr, y_scr) = refs
        h0_ref = None
    else:
        (z_ref, xa_ref, dtr_ref, tril_ref, rm_ref, dtb_ref, a_ref, ex_ref, ext_ref, dsk_ref, ng_ref, h0_ref,
         y_ref, ho_ref, h_scr, acs_scr, xdt_scr, c_scr, b_scr, y_scr) = refs
    ci = pl.program_id(0)
    sub = pl.program_id(1)
    R = SSD_CHUNK
    GW = SSD_HPG * SSD_HEAD_DIM

    if carry:
        @pl.when(ci % chunks_per_seq == 0)
        def _():
            h_scr[...] = jnp.zeros_like(h_scr)

    @pl.when(sub == 0)
    def _():
        trilf = tril_ref[...]
        dt = jax.nn.softplus(dtr_ref[...] + dtb_ref[...])
        a_cs = _dot_exact(trilf, dt * a_ref[...])
        acs_scr[...] = a_cs
        a_cs_t = a_cs.T
        xs = xa_ref[:, 0:SSD_INNER]
        xdt = xs * _dot_exact(dt, ex_ref[...])
        xdt_scr[...] = xdt
        y_scr[...] = dsk_ref[...] * xs
        for g in range(SSD_GROUPS):
            bm = xa_ref[:, SSD_INNER + g * SSD_STATE:SSD_INNER + (g + 1) * SSD_STATE]
            cm = xa_ref[:, SSD_INNER + SSD_GN + g * SSD_STATE:SSD_INNER + SSD_GN + (g + 1) * SSD_STATE]
            b_scr[:, g * SSD_STATE:(g + 1) * SSD_STATE] = bm
            cmb = cm.astype(BF16)
            c_scr[:, g * SSD_STATE:(g + 1) * SSD_STATE] = cmb
            cb = _dot_nt(cmb, bm.astype(BF16)) * trilf
            for jh in range(SSD_HPG):
                hd = g * SSD_HPG + jh
                lanes = slice(hd * SSD_HEAD_DIM, (hd + 1) * SSD_HEAD_DIM)
                seg = a_cs[:, hd:hd + 1] - a_cs_t[hd:hd + 1, :]
                lmat = (cb * jnp.exp(jnp.minimum(seg, 0.0))).astype(BF16)
                y_scr[:, lanes] += _dot(lmat, xdt[:, lanes].astype(BF16))

    rm = rm_ref[sub]
    a_cs = acs_scr[...]
    a_last = acs_scr[pl.ds(sub * Lc + Lc - 1, 1), :]
    e_in = _dot_exact(jnp.exp(a_cs) * rm, ex_ref[...])
    d_end = jnp.exp(jnp.minimum(a_last - a_cs, 0.0)) * rm
    xw = xdt_scr[...] * _dot_exact(d_end, ex_ref[...])
    cd = _dot_exact(ext_ref[...], jnp.broadcast_to(jnp.exp(a_last), (R, SSD_STATE)).T)
    for g in range(SSD_GROUPS):
        rows = slice(g * GW, (g + 1) * GW)
        if carry:
            hg = h_scr[rows, :]
        else:
            hg = h0_ref[0, rows, :]
        yoff = _dot_nt(c_scr[:, g * SSD_STATE:(g + 1) * SSD_STATE], hg.astype(BF16))
        y_scr[:, rows] += e_in[:, rows] * yoff
        st = _dot(xw[:, rows].T.astype(BF16), b_scr[:, g * SSD_STATE:(g + 1) * SSD_STATE].astype(BF16))
        hn = hg * cd[rows, :] + st
        if carry:
            h_scr[rows, :] = hn
        else:
            ho_ref[0, rows, :] = hn

    if carry:
        @pl.when(ci % chunks_per_seq == chunks_per_seq - 1)
        def _():
            ho_ref[0] = h_scr[...]

    @pl.when(sub == nsub - 1)
    def _():
        z = z_ref[...]
        yg = y_scr[...] * (z * jax.nn.sigmoid(z))
        for g in range(SSD_GROUPS):
            lanes = slice(g * GW, (g + 1) * GW)
            t = yg[:, lanes]
            r = lax.rsqrt(jnp.mean(t * t, axis=-1, keepdims=True) + RMS_EPS)
            y_ref[:, lanes] = t * r * ng_ref[:, lanes]


def _ssd_core(z, xa, dtr, h0, dt_bias, a_log, d_skip, norm_g, B, L):
    M = z.shape[0]
    R = SSD_CHUNK
    carry = L >= R
    if carry:
        assert L % R == 0 and h0 is None
        nsub, Lc, cps = 1, R, L // R
    else:
        assert R % L == 0 and M % R == 0
        nsub, Lc, cps = R // L, L, 1
    nchunk = M // R
    r = jnp.arange(R)
    same = (r[:, None] // Lc) == (r[None, :] // Lc)
    tril = (same & (r[None, :] <= r[:, None])).astype(F32)
    rm = ((r[None, :, None] // Lc) == jnp.arange(nsub)[:, None, None]).astype(F32)
    rm = jnp.broadcast_to(rm, (nsub, R, SSD_STATE))
    hsel = (jnp.arange(SSD_STATE)[:, None] == (jnp.arange(SSD_INNER)[None, :] // SSD_HEAD_DIM)).astype(F32)
    pad = SSD_STATE - SSD_HEADS
    dtb = jnp.pad(dt_bias, (0, pad)).reshape(1, SSD_STATE)
    a = jnp.pad(-jnp.exp(a_log), (0, pad)).reshape(1, SSD_STATE)
    dsk = jnp.repeat(d_skip, SSD_HEAD_DIM).reshape(1, SSD_INNER)
    HS = SSD_HEADS * SSD_HEAD_DIM
    const = lambda *shape: pl.BlockSpec(shape, lambda c, s: (0,) * len(shape))
    in_specs = [pl.BlockSpec((R, SSD_INNER), lambda c, s: (c, 0)),
                pl.BlockSpec((R, SSD_CONV_DIM), lambda c, s: (c, 0)),
                pl.BlockSpec((R, SSD_STATE), lambda c, s: (c, 0)),
                const(R, R), const(nsub, R, SSD_STATE), const(1, SSD_STATE), const(1, SSD_STATE),
                const(SSD_STATE, SSD_INNER), const(SSD_INNER, SSD_STATE), const(1, SSD_INNER),
                const(1, SSD_INNER)]
    args = [z, xa, dtr, tril, rm, dtb, a, hsel, hsel.T, dsk, norm_g.reshape(1, SSD_INNER)]
    if carry:
        ho_spec = pl.BlockSpec((1, HS, SSD_STATE), lambda c, s: (c // cps, 0, 0))
    else:
        in_specs.append(pl.BlockSpec((1, HS, SSD_STATE), lambda c, s: (c * nsub + s, 0, 0)))
        args.append(h0.reshape(B, HS, SSD_STATE))
        ho_spec = pl.BlockSpec((1, HS, SSD_STATE), lambda c, s: (c * nsub + s, 0, 0))
    return pl.pallas_call(
        functools.partial(_ssd_kernel, nsub=nsub, Lc=Lc, carry=carry, chunks_per_seq=cps),
        grid=(nchunk, nsub),
        in_specs=in_specs,
        out_specs=[pl.BlockSpec((R, SSD_INNER), lambda c, s: (c, 0)), ho_spec],
        out_shape=[jax.ShapeDtypeStruct((M, SSD_INNER), F32),
                   jax.ShapeDtypeStruct((B, HS, SSD_STATE), F32)],
        scratch_shapes=[pltpu.VMEM((HS, SSD_STATE), F32),
                        pltpu.VMEM((R, SSD_STATE), F32),
                        pltpu.VMEM((R, SSD_INNER), F32),
                        pltpu.VMEM((R, SSD_GN), BF16),
                        pltpu.VMEM((R, SSD_GN), F32),
                        pltpu.VMEM((R, SSD_INNER), F32)],
        compiler_params=_cparams(("arbitrary", "arbitrary"), VMEM_LIMIT_BYTES),
        name="ssd_core",
    )(*args)


def _compare_exchange(xs, i, j):
    a, b = xs[i], xs[j]
    xs[i], xs[j] = jnp.maximum(a, b), jnp.minimum(a, b)


def _bitonic_sort_desc(xs):
    n = len(xs)
    k = 2
    while k <= n:
        j = k // 2
        while j >= 1:
            for i in range(n):
                p = i ^ j
                if p > i:
                    if (i & k) == 0:
                        _compare_exchange(xs, i, p)
                    else:
                        _compare_exchange(xs, p, i)
            j //= 2
        k *= 2


def _bitonic_merge_desc(xs):
    j = len(xs) // 2
    while j >= 1:
        for i in range(len(xs)):
            p = i ^ j
            if p > i:
                _compare_exchange(xs, i, p)
        j //= 2


def _top16_of_128(s):
    xs = [s[SUBLANES * i:SUBLANES * (i + 1), :] for i in range(PEER_TOPK)]
    _bitonic_sort_desc(xs)
    dropped = jnp.full(xs[0].shape, -jnp.inf, F32)
    for shift in (4, 2, 1):
        ys = [pltpu.roll(xs[PEER_TOPK - 1 - i], shift, 0) for i in range(PEER_TOPK)]
        lost = functools.reduce(jnp.maximum, [jnp.minimum(a, b) for a, b in zip(xs, ys)])
        dropped = jnp.maximum(jnp.maximum(dropped, pltpu.roll(dropped, shift, 0)), lost)
        xs = [jnp.maximum(a, b) for a, b in zip(xs, ys)]
        _bitonic_merge_desc(xs)
    return xs, dropped


def _peer_route_kernel(x_ref, wq_ref, keys_ref, r2_ref, e2_ref, n1_ref, c1_ref, s_scr, m_scr):
    Tq = x_ref.shape[0]
    NEG = -jnp.inf
    q = _dot(x_ref[...].astype(BF16), wq_ref[...]).astype(BF16)
    for h in range(PEER_HEADS):
        for half in range(2):
            col = (h * 2 + half) * PEER_HALF
            s = _dot_nt(keys_ref[h, half], q[:, col:col + PEER_HALF])
            s_scr[half, h] = s
            top, nxt = _top16_of_128(s)
            for r in range(PEER_TOPK):
                m_scr[half, r, h:h + 1, :] = top[r][0:1, :]
            m_scr[half, PEER_TOPK, h:h + 1, :] = nxt[0:1, :]

    pairs = [(r, c) for r in range(PEER_ROUNDS) for c in range(PEER_ROUNDS)
             if (r + 1) * (c + 1) <= PEER_ROUNDS]
    sums = [m_scr[0, r] + m_scr[1, c] for r, c in pairs]
    cum = jnp.zeros((PEER_HEADS, Tq), F32)
    v16 = jnp.full((PEER_HEADS, Tq), NEG, F32)
    v17 = jnp.full((PEER_HEADS, Tq), NEG, F32)
    for _ in range(PEER_ROUNDS):
        mx = functools.reduce(jnp.maximum, sums)
        hit = [sv == mx for sv in sums]
        new = cum + functools.reduce(lambda p, t: p + t, [jnp.where(hm, 1.0, 0.0) for hm in hit])
        v16 = jnp.where((cum < PEER_TOPK) & (new >= PEER_TOPK), mx, v16)
        v17 = jnp.where((cum < PEER_TOPK + 1) & (new >= PEER_TOPK + 1), mx, v17)
        sums = [jnp.where(hm, NEG, sv) for hm, sv in zip(hit, sums)]
        cum = new
    tau = 0.5 * (v16 + v17)
    m1, m2 = m_scr[0, 0], m_scr[1, 0]
    zsum = jnp.zeros((PEER_HEADS, Tq), F32)
    for r, c in pairs:
        if r < PEER_TOPK and c < PEER_TOPK:
            a1, a2 = m_scr[0, r], m_scr[1, c]
            zsum = zsum + jnp.where(a2 >= tau - a1, jnp.exp(a1 - m1) * jnp.exp(a2 - m2), 0.0)
    zinv = 1.0 / zsum
    for h in range(PEER_HEADS):
        s1, s2 = s_scr[0, h], s_scr[1, h]
        thr = tau[h:h + 1, :] - s1
        rank2 = jnp.zeros((PEER_NKEYS, Tq), F32)
        count1 = jnp.zeros((PEER_NKEYS, Tq), F32)
        for r in range(PEER_TOPK):
            a2 = m_scr[1, r, h:h + 1, :]
            rank2 = jnp.where(a2 > s2, r + 1.0, rank2)
            count1 = jnp.where(a2 >= thr, r + 1.0, count1)
        r2_ref[h] = rank2
        n1_ref[h] = count1
        c1_ref[h] = jnp.exp(s1 - m1[h:h + 1, :]) * zinv[h:h + 1, :]
        e2_ref[h] = jnp.exp(s2 - m2[h:h + 1, :])


def _peer_route(hn, wq, keys):
    T = hn.shape[0]
    Tq = min(PEER_TQ, T)
    assert T % Tq == 0
    shape = (PEER_HEADS, PEER_NKEYS, T)
    ospec = pl.BlockSpec((PEER_HEADS, PEER_NKEYS, Tq), lambda i: (0, 0, i))
    return pl.pallas_call(
        _peer_route_kernel,
        grid=(T // Tq,),
        in_specs=[pl.BlockSpec((Tq, D_MODEL), lambda i: (i, 0)),
                  pl.BlockSpec(wq.shape, lambda i: (0, 0)),
                  pl.BlockSpec(keys.shape, lambda i: (0, 0, 0, 0))],
        out_specs=[ospec] * 4,
        out_shape=[jax.ShapeDtypeStruct(shape, F32)] * 4,
        scratch_shapes=[pltpu.VMEM((2, PEER_HEADS, PEER_NKEYS, Tq), F32),
                        pltpu.VMEM((2, PEER_ROUNDS, PEER_HEADS, Tq), F32)],
        compiler_params=_cparams(("parallel",), VMEM_LIMIT_BYTES),
        name="peer_route",
    )(hn, wq, keys)


def _peer_ffn_kernel(x_ref, u_ref, vt_ref, r2_ref, e2_ref, n1_ref, c1_ref, g_ref, b_ref,
                     o_ref, xb_scr, s_scr, p_scr, acc_scr, r2b_scr, e2b_scr):
    c = pl.program_id(1)
    Tb = x_ref.shape[0]
    ni = PEER_EC // PEER_NKEYS
    assert ni == SUBLANES
    bf16_rows = 2 * SUBLANES

    @pl.when(c == 0)
    def _():
        xb_scr[...] = x_ref[...].astype(BF16)
        acc_scr[...] = jnp.zeros_like(acc_scr)
        for h in range(PEER_HEADS):
            r2b_scr[h] = r2_ref[h].astype(BF16)
            e2b_scr[h] = e2_ref[h].astype(BF16)

    s_scr[...] = _dot_nt(u_ref[...], xb_scr[...])

    irows = pl.ds(pl.multiple_of(c * ni, ni), ni)

    def i_body(ii, carry):
        rows = pl.ds(pl.multiple_of(ii * PEER_NKEYS, PEER_NKEYS), PEER_NKEYS)

        def row_tile(ref, h, lanes):
            top = pltpu.roll(ref[h, irows, lanes], ni - ii, 0)[0:1, :]
            r = jnp.broadcast_to(top, (bf16_rows, LANES)).astype(BF16)
            return jnp.tile(r, (PEER_NKEYS // bf16_rows, 1))

        for tt in range(Tb // LANES):
            lanes = slice(tt * LANES, (tt + 1) * LANES)
            gate = jnp.zeros((PEER_NKEYS, LANES), BF16)
            for h in range(PEER_HEADS):
                sel = r2b_scr[h, :, lanes] < row_tile(n1_ref, h, lanes)
                gate = gate + jnp.where(sel, e2b_scr[h, :, lanes] * row_tile(c1_ref, h, lanes), 0)
            p_scr[rows, lanes] = _gelu(s_scr[rows, lanes]).astype(BF16) * gate
        return carry

    lax.fori_loop(0, ni, i_body, 0)
    acc_scr[...] += _dot(vt_ref[...], p_scr[...])

    @pl.when(c == pl.num_programs(1) - 1)
    def _():
        o_ref[...] = _ln(ALPHA * x_ref[...] + acc_scr[...].T, g_ref[...], b_ref[...])


def _peer_ffn(hn, u, vt, r2, e2, n1, c1, g, b):
    T = hn.shape[0]
    E = u.shape[0]
    Tb = min(PEER_TB, T)
    assert T % Tb == 0 and E % PEER_EC == 0
    rspec = pl.BlockSpec((PEER_HEADS, PEER_NKEYS, Tb), lambda i, c: (0, 0, i))
    return pl.pallas_call(
        _peer_ffn_kernel,
        grid=(T // Tb, E // PEER_EC),
        in_specs=[pl.BlockSpec((Tb, D_MODEL), lambda i, c: (i, 0)),
                  pl.BlockSpec((PEER_EC, D_MODEL), lambda i, c: (c, 0)),
                  pl.BlockSpec((D_MODEL, PEER_EC), lambda i, c: (0, c)),
                  rspec, rspec, rspec, rspec,
                  pl.BlockSpec((1, D_MODEL), lambda i, c: (0, 0)),
                  pl.BlockSpec((1, D_MODEL), lambda i, c: (0, 0))],
        out_specs=pl.BlockSpec((Tb, D_MODEL), lambda i, c: (i, 0)),
        out_shape=jax.ShapeDtypeStruct((T, D_MODEL), F32),
        scratch_shapes=[pltpu.VMEM((Tb, D_MODEL), BF16),
                        pltpu.VMEM((PEER_EC, Tb), F32),
                        pltpu.VMEM((PEER_EC, Tb), BF16),
                        pltpu.VMEM((D_MODEL, Tb), F32),
                        pltpu.VMEM((PEER_HEADS, PEER_NKEYS, Tb), BF16),
                        pltpu.VMEM((PEER_HEADS, PEER_NKEYS, Tb), BF16)],
        compiler_params=_cparams(("parallel", "arbitrary"), VMEM_LIMIT_BYTES),
        name="peer_ffn",
    )(hn, u, vt, r2, e2, n1, c1, g.reshape(1, -1), b.reshape(1, -1))


def _trunk(x, pos0, s5_re, s5_im, pool_buf, conv_buf, ssd_h, W, need_v):
    B, L, _ = x.shape
    T = B * L
    h = x.reshape(T, D_MODEL)
    to_tm = lambda a: a.reshape(B, L, -1).transpose(1, 0, 2).reshape(T, -1)
    to_bm = lambda a: a.reshape(L, B, -1).transpose(1, 0, 2).reshape(T, -1)
    cmlp_v = None
    for i in range(DEPTH):
        kind = i % 4
        if kind == 0:
            u = _mm(h, W["s5_w_in"])
            g_tm, hl = _s5_core(to_tm(u), _s5_state_to_blocks(s5_re, s5_im), W["s5_wbb"], W["s5_wc"],
                                W["s5_a"], W["s5_d"], B, L)
            s5_re, s5_im = _s5_blocks_to_state(hl)
            g = to_bm(g_tm)
            mid = _mm(g, W["s5_w_glu"], bias=W["s5_b_glu"], gate=g)
            w_out = W["s5_w_out"]
        elif kind == 1:
            u = _mm(h, W["pool_w_in"])
            mid = _causal_group(u, pool_buf, W["pool_w_grp"], W["pool_scale"].reshape(1, -1),
                                mode="pool", B=B, L=L, pos0=pos0, n_hist=POOL_BUF)
            pool_buf = jnp.concatenate([pool_buf, u.reshape(B, L, -1)], axis=1)[:, -POOL_BUF:]
            w_out = W["pool_w_out"]
        elif kind == 2:
            z = _mm(h, W["cmlp_w_in"], bias=W["cmlp_b_in"], act="gelu")
            ws_eff, bias_eff = _cmlp_mix_params(W["cmlp_w_s"], W["cmlp_b_s"], L)
            outs = _cmlp_core(z, ws_eff, bias_eff, W["cmlp_ln_g"], W["cmlp_ln_b"], write_v=need_v)
            mid = outs[0]
            if need_v:
                cmlp_v = outs[1].reshape(B, L, CMLP_WIDTH)
            w_out = W["cmlp_w_out"]
        else:
            zg = _mm(h, W["ssd_w_z"])
            xbc = _mm(h, W["ssd_w_xbc"])
            dtr = _mm(h, W["ssd_w_dt"])
            xa = _causal_group(xbc, conv_buf, W["ssd_conv_w"], W["ssd_conv_b"].reshape(1, -1),
                               mode="conv", B=B, L=L, pos0=pos0, n_hist=SSD_CONV - 1)
            conv_buf = jnp.concatenate([conv_buf, xbc.reshape(B, L, -1)], axis=1)[:, -(SSD_CONV - 1):]
            mid, hs = _ssd_core(zg, xa, dtr, ssd_h, W["ssd_dt_bias"], W["ssd_a_log"], W["ssd_d"],
                                W["ssd_norm_g"], B, L)
            ssd_h = hs.reshape(B, SSD_HEADS, SSD_HEAD_DIM, SSD_STATE)
            w_out = W["ssd_w_out"]
        h = _mm_ln(mid, w_out, h, W["ln1_g"][i], W["ln1_b"][i])
        r2, e2, n1, c1 = _peer_route(h, W["peer_w_q"][i], W["peer_keys"][i])
        h = _peer_ffn(h, W["peer_u"][i], W["peer_vt"][i], r2, e2, n1, c1, W["ln2_g"][i], W["ln2_b"][i])
    return h.reshape(B, L, D_MODEL), s5_re, s5_im, pool_buf, cmlp_v, conv_buf, ssd_h


def kernel(x_prompt, x_sample, state_s5_re, state_s5_im, state_pool, state_ssd_conv, state_ssd, s5_w_in, s5_a_re, s5_a_im, s5_log_dt, s5_b_re, s5_b_im, s5_c_re, s5_c_im, s5_d, s5_w_glu, s5_b_glu, s5_w_out, pool_w_in, pool_w_grp, pool_scale, pool_w_out, cmlp_w_in, cmlp_b_in, cmlp_ln_g, cmlp_ln_b, cmlp_w_s, cmlp_b_s, cmlp_w_out, ssd_w_in, ssd_conv_w, ssd_conv_b, ssd_dt_bias, ssd_a_log, ssd_d, ssd_norm_g, ssd_w_out, ln1_g, ln1_b, ln2_g, ln2_b, peer_w_q, peer_keys, peer_u, peer_v):
    bf = lambda a: a.astype(BF16)
    wbb, wc, a_blk = _s5_params(s5_a_re, s5_a_im, s5_log_dt, s5_b_re, s5_b_im, s5_c_re, s5_c_im)
    dt_cols = SSD_STATE - SSD_HEADS
    W = dict(
        s5_w_in=bf(s5_w_in), s5_wbb=wbb, s5_wc=wc, s5_a=a_blk, s5_d=s5_d,
        s5_w_glu=bf(s5_w_glu), s5_b_glu=s5_b_glu, s5_w_out=bf(s5_w_out),
        pool_w_in=bf(pool_w_in), pool_w_grp=bf(pool_w_grp), pool_scale=pool_scale, pool_w_out=bf(pool_w_out),
        cmlp_w_in=bf(cmlp_w_in), cmlp_b_in=cmlp_b_in, cmlp_ln_g=cmlp_ln_g, cmlp_ln_b=cmlp_ln_b,
        cmlp_w_s=cmlp_w_s, cmlp_b_s=cmlp_b_s, cmlp_w_out=bf(cmlp_w_out),
        ssd_w_z=bf(ssd_w_in[:, :SSD_INNER]),
        ssd_w_xbc=bf(ssd_w_in[:, SSD_INNER:SSD_INNER + SSD_CONV_DIM]),
        ssd_w_dt=bf(jnp.pad(ssd_w_in[:, SSD_INNER + SSD_CONV_DIM:], ((0, 0), (0, dt_cols)))),
        ssd_conv_w=ssd_conv_w, ssd_conv_b=ssd_conv_b, ssd_dt_bias=ssd_dt_bias, ssd_a_log=ssd_a_log,
        ssd_d=ssd_d, ssd_norm_g=ssd_norm_g, ssd_w_out=bf(ssd_w_out),
        ln1_g=ln1_g, ln1_b=ln1_b, ln2_g=ln2_g, ln2_b=ln2_b,
        peer_w_q=bf(peer_w_q), peer_keys=bf(peer_keys), peer_u=bf(peer_u),
        peer_vt=bf(peer_v).transpose(0, 2, 1),
    )
    bp = x_prompt.shape[0]
    zeros = lambda *s: jnp.zeros(s, F32)
    (y_p, s5_re_p, s5_im_p, pool_p, _, conv_p, ssd_p) = _trunk(
        x_prompt, 0, zeros(bp, S5_GROUPS, S5_STATE), zeros(bp, S5_GROUPS, S5_STATE),
        zeros(bp, POOL_BUF, D_MODEL), zeros(bp, SSD_CONV - 1, SSD_CONV_DIM), None, W, False)
    past_len = 16384
    (y_s, s5_re_s, s5_im_s, pool_s, cmlp_v_s, conv_s, ssd_s) = _trunk(
        x_sample, past_len, state_s5_re, state_s5_im, state_pool, state_ssd_conv, state_ssd, W, True)
    return (y_p, y_s, s5_re_p, s5_im_p, pool_p, conv_p, ssd_p,
            s5_re_s, s5_im_s, pool_s, cmlp_v_s, conv_s, ssd_s)
```

```python
import functools
import math

import jax
import jax.numpy as jnp
from jax import lax
from jax.experimental import pallas as pl
from jax.experimental.pallas import tpu as pltpu

F32 = jnp.float32
BF16 = jnp.bfloat16

D_MODEL = 1024
DEPTH = 4
ALPHA = (2 * DEPTH) ** 0.25
LN_EPS = 1e-5
RMS_EPS = 1e-5

S5_GROUPS = 64
S5_GROUP = 16
S5_STATE = 64
S5_GB = 8
S5_ROWS = 1024

POOL_WINDOWS = (2, 4, 8, 16)
POOL_GROUP = 256
POOL_BUF = 15
POOL_HIST = 16

CMLP_WIDTH = 1024
CMLP_CHUNK = 128
CMLP_HEADS = 4
CMLP_HEAD_DIM = 256

SSD_INNER = 2048
SSD_HEAD_DIM = 64
SSD_HEADS = 32
SSD_STATE = 128
SSD_GROUPS = 4
SSD_HPG = SSD_HEADS // SSD_GROUPS
SSD_CONV = 4
SSD_CHUNK = 128
SSD_CONV_DIM = SSD_INNER + 2 * SSD_GROUPS * SSD_STATE
SSD_GN = SSD_GROUPS * SSD_STATE

PEER_HEADS = 8
PEER_NKEYS = 128
PEER_TOPK = 16
PEER_HALF = 128
PEER_ROUNDS = PEER_TOPK + 1
PEER_TQ = 256
PEER_TB = 512
PEER_EC = 1024

GELU_K0 = -2.0 * math.sqrt(2.0 / math.pi) * math.log2(math.e)
GELU_K1 = 0.044715 * GELU_K0

LANES = 128
SUBLANES = 8
VMEM_LIMIT_BYTES = 56 * 1024 * 1024


def _cparams(semantics, vmem=None):
    return pltpu.CompilerParams(dimension_semantics=semantics, vmem_limit_bytes=vmem)


def _ln(y, g, b):
    mu = jnp.mean(y, axis=-1, keepdims=True)
    yc = y - mu
    var = jnp.mean(yc * yc, axis=-1, keepdims=True)
    return yc * lax.rsqrt(var + LN_EPS) * g + b


def _gelu(x):
    w = x * (GELU_K0 + GELU_K1 * (x * x))
    return x / (1.0 + jnp.exp2(w))


def _dot(a, b):
    return jnp.dot(a, b, preferred_element_type=F32)


def _dot_nt(a, b):
    return lax.dot_general(a, b, (((1,), (1,)), ((), ())), preferred_element_type=F32)


def _dot_exact(a, b):
    return jnp.dot(a, b, preferred_element_type=F32, precision=lax.Precision.HIGHEST)


def _split_bf16(a):
    hi = a.astype(BF16)
    return hi, (a - hi.astype(F32)).astype(BF16)


def _dot_sel(a, sel):
    hi, lo = _split_bf16(a)
    return _dot(hi, sel) + _dot(lo, sel)


def _sel_dot(sel, b):
    hi, lo = _split_bf16(b)
    return _dot(sel, hi) + _dot(sel, lo)


def _mm_kernel(*refs, act, has_bias, has_gate):
    x_ref, w_ref = refs[0], refs[1]
    k = 2
    acc = _dot(x_ref[...].astype(BF16), w_ref[...])
    if has_bias:
        acc = acc + refs[k][...]
        k += 1
    if act == "gelu":
        acc = _gelu(acc)
    if has_gate:
        acc = refs[k][...] * jax.nn.sigmoid(acc)
        k += 1
    refs[k][...] = acc


def _mm(x, w, bias=None, act=None, gate=None, tm=512, tn=512):
    M, K = x.shape
    N = w.shape[1]
    tm, tn = min(tm, M), min(tn, N)
    assert M % tm == 0 and N % tn == 0
    in_specs = [pl.BlockSpec((tm, K), lambda i, j: (i, 0)),
                pl.BlockSpec((K, tn), lambda i, j: (0, j))]
    args = [x, w]
    if bias is not None:
        in_specs.append(pl.BlockSpec((1, tn), lambda i, j: (0, j)))
        args.append(bias.reshape(1, N))
    if gate is not None:
        in_specs.append(pl.BlockSpec((tm, tn), lambda i, j: (i, j)))
        args.append(gate)
    return pl.pallas_call(
        functools.partial(_mm_kernel, act=act, has_bias=bias is not None, has_gate=gate is not None),
        grid=(M // tm, N // tn),
        in_specs=in_specs,
        out_specs=pl.BlockSpec((tm, tn), lambda i, j: (i, j)),
        out_shape=jax.ShapeDtypeStruct((M, N), F32),
        compiler_params=_cparams(("parallel", "parallel"), VMEM_LIMIT_BYTES),
        name="mm",
    )(*args)


def _mm_ln_kernel(x_ref, w_ref, h_ref, g_ref, b_ref, o_ref):
    acc = _dot(x_ref[...].astype(BF16), w_ref[...])
    o_ref[...] = _ln(ALPHA * h_ref[...] + acc, g_ref[...], b_ref[...])


def _mm_ln(x, w, h, g, b, tm=256):
    M, K = x.shape
    N = w.shape[1]
    tm = min(tm, M)
    assert M % tm == 0
    return pl.pallas_call(
        _mm_ln_kernel,
        grid=(M // tm,),
        in_specs=[pl.BlockSpec((tm, K), lambda i: (i, 0)),
                  pl.BlockSpec((K, N), lambda i: (0, 0)),
                  pl.BlockSpec((tm, N), lambda i: (i, 0)),
                  pl.BlockSpec((1, N), lambda i: (0, 0)),
                  pl.BlockSpec((1, N), lambda i: (0, 0))],
        out_specs=pl.BlockSpec((tm, N), lambda i: (i, 0)),
        out_shape=jax.ShapeDtypeStruct((M, N), F32),
        compiler_params=_cparams(("parallel",), VMEM_LIMIT_BYTES),
        name="mm_ln",
    )(x, w, h, g.reshape(1, N), b.reshape(1, N))


def _s5_kernel(u_ref, wbb_ref, wc_ref, a_ref, d_ref, h0_ref, g_ref, hl_ref,
               h_scr, bu_scr, hs_scr, *, B, Tc):
    tc = pl.program_id(1)
    P = S5_GB * S5_STATE

    @pl.when(tc == 0)
    def _():
        h_scr[...] = h0_ref[0]

    u = u_ref[...]
    bu_scr[...] = _dot(u.astype(BF16), wbb_ref[0])
    ar = jnp.broadcast_to(a_ref[0, 0:1, :], (SUBLANES, P))
    ai = jnp.broadcast_to(a_ref[0, 1:2, :], (SUBLANES, P))

    def seq_body(sb, carry):
        r0 = pl.multiple_of(sb * SUBLANES, SUBLANES)

        def t_body(t, h):
            hr, hi = h
            rows = pl.ds(pl.multiple_of(t * B + r0, SUBLANES), SUBLANES)
            nr = ar * hr - ai * hi + bu_scr[rows, 0:P]
            ni = ar * hi + ai * hr + bu_scr[rows, P:2 * P]
            hs_scr[rows, 0:P] = nr
            hs_scr[rows, P:2 * P] = ni
            return nr, ni

        hr, hi = lax.fori_loop(0, Tc, t_body,
                               (h_scr[pl.ds(r0, SUBLANES), 0:P], h_scr[pl.ds(r0, SUBLANES), P:2 * P]))
        h_scr[pl.ds(r0, SUBLANES), 0:P] = hr
        h_scr[pl.ds(r0, SUBLANES), P:2 * P] = hi
        return carry

    lax.fori_loop(0, B // SUBLANES, seq_body, 0)
    y = _dot(hs_scr[...].astype(BF16), wc_ref[0]) + d_ref[...] * u
    g_ref[...] = _gelu(y)

    @pl.when(tc == pl.num_programs(1) - 1)
    def _():
        hl_ref[0] = h_scr[...]


def _s5_core(u_tm, h0_blk, wbb, wc, a_blk, d_skip, B, L):
    Tc = S5_ROWS // B
    assert B % SUBLANES == 0 and L % Tc == 0
    P2 = 2 * S5_GB * S5_STATE
    W = S5_GB * S5_GROUP
    n_gb = S5_GROUPS // S5_GB
    return pl.pallas_call(
        functools.partial(_s5_kernel, B=B, Tc=Tc),
        grid=(n_gb, L // Tc),
        in_specs=[pl.BlockSpec((S5_ROWS, W), lambda g, t: (t, g)),
                  pl.BlockSpec((1, W, P2), lambda g, t: (g, 0, 0)),
                  pl.BlockSpec((1, P2, W), lambda g, t: (g, 0, 0)),
                  pl.BlockSpec((1, 2, P2 // 2), lambda g, t: (g, 0, 0)),
                  pl.BlockSpec((1, W), lambda g, t: (0, g)),
                  pl.BlockSpec((1, B, P2), lambda g, t: (g, 0, 0))],
        out_specs=[pl.BlockSpec((S5_ROWS, W), lambda g, t: (t, g)),
                   pl.BlockSpec((1, B, P2), lambda g, t: (g, 0, 0))],
        out_shape=[jax.ShapeDtypeStruct((L * B, D_MODEL), F32),
                   jax.ShapeDtypeStruct((n_gb, B, P2), F32)],
        scratch_shapes=[pltpu.VMEM((B, P2), F32),
                        pltpu.VMEM((S5_ROWS, P2), F32),
                        pltpu.VMEM((S5_ROWS, P2), F32)],
        compiler_params=_cparams(("parallel", "arbitrary"), VMEM_LIMIT_BYTES),
        name="s5_core",
    )(u_tm, wbb, wc, a_blk, d_skip.reshape(1, D_MODEL), h0_blk)


def _s5_params(a_re, a_im, log_dt, b_re, b_im, c_re, c_im):
    dt = jnp.exp(log_dt)[:, None]
    mag = jnp.exp(a_re * dt)
    lb_r, lb_i = mag * jnp.cos(a_im * dt), mag * jnp.sin(a_im * dt)
    den = a_re * a_re + a_im * a_im
    f_r = ((lb_r - 1.0) * a_re + lb_i * a_im) / den
    f_i = (lb_i * a_re - (lb_r - 1.0) * a_im) / den
    bb_r = f_r[..., None] * b_re - f_i[..., None] * b_im
    bb_i = f_r[..., None] * b_im + f_i[..., None] * b_re
    n_gb = S5_GROUPS // S5_GB
    eye = jnp.eye(S5_GB, dtype=F32)

    def blockdiag_in(bb):
        t = bb.reshape(n_gb, S5_GB, S5_STATE, S5_GROUP).transpose(0, 1, 3, 2)
        return jnp.einsum("bgip,gh->bgihp", t, eye).reshape(n_gb, S5_GB * S5_GROUP, S5_GB * S5_STATE)

    def blockdiag_out(c):
        t = c.reshape(n_gb, S5_GB, S5_GROUP, S5_STATE).transpose(0, 1, 3, 2)
        return jnp.einsum("bgpi,gh->bgphi", t, eye).reshape(n_gb, S5_GB * S5_STATE, S5_GB * S5_GROUP)

    wbb = jnp.concatenate([blockdiag_in(bb_r), blockdiag_in(bb_i)], axis=-1).astype(BF16)
    wc = jnp.concatenate([blockdiag_out(c_re), -blockdiag_out(c_im)], axis=1).astype(BF16)
    a_blk = jnp.stack([lb_r.reshape(n_gb, -1), lb_i.reshape(n_gb, -1)], axis=1)
    return wbb, wc, a_blk


def _s5_state_to_blocks(h_re, h_im):
    B = h_re.shape[0]
    n_gb = S5_GROUPS // S5_GB
    r = h_re.reshape(B, n_gb, -1).transpose(1, 0, 2)
    i = h_im.reshape(B, n_gb, -1).transpose(1, 0, 2)
    return jnp.concatenate([r, i], axis=-1)


def _s5_blocks_to_state(h_blk):
    n_gb, B, P2 = h_blk.shape
    r = h_blk[..., :P2 // 2].transpose(1, 0, 2).reshape(B, S5_GROUPS, S5_STATE)
    i = h_blk[..., P2 // 2:].transpose(1, 0, 2).reshape(B, S5_GROUPS, S5_STATE)
    return r, i


def _causal_kernel(x_ref, st_ref, p1_ref, p2_ref, o_ref, ctx, *, mode, S, H, Tb, pos0, ntime):
    j = pl.program_id(2)
    HS, TS = H * S, Tb * S

    @pl.when(j == 0)
    def _():
        ctx[0:HS, :] = st_ref[0]

    ctx[HS:HS + TS, :] = x_ref[...]

    def back(k, lanes):
        return ctx[HS - k * S:HS - k * S + TS, lanes]

    if mode == "conv":
        acc = p2_ref[...] + back(0, slice(None)) * p1_ref[SSD_CONV - 1:SSD_CONV, :]
        for k in range(1, SSD_CONV):
            acc = acc + back(k, slice(None)) * p1_ref[SSD_CONV - 1 - k:SSD_CONV - k, :]
        o_ref[...] = acc * jax.nn.sigmoid(acc)
    else:
        for gi, w in enumerate(POOL_WINDOWS):
            lanes = slice(gi * POOL_GROUP, (gi + 1) * POOL_GROUP)
            cur = back(0, lanes)
            win = cur
            for k in range(1, w):
                win = win + back(k, lanes)
            if pos0 + 1 >= w:
                mean = win * (1.0 / w)
            else:
                assert S == 1, "position-dependent window counts need step-contiguous rows"
                t = lax.broadcasted_iota(jnp.int32, (TS, POOL_GROUP), 0)
                cnt = jnp.minimum(pos0 + j * Tb + t + 1, w).astype(F32)
                mean = win / cnt
            mixed = _dot((mean - cur).astype(BF16), p1_ref[gi])
            o_ref[:, lanes] = mixed * p2_ref[:, lanes]

    if ntime > 1:
        ctx[0:HS, :] = ctx[TS:TS + HS, :]


def _causal(x, st, p1, p2, *, mode, nseq, S, H, Tb, L, pos0):
    C = x.shape[1]
    ncol = C // D_MODEL
    ntime = L // Tb
    assert L % Tb == 0 and (H * S) % SUBLANES == 0
    if mode == "conv":
        p1_spec = pl.BlockSpec((SSD_CONV, D_MODEL), lambda s, c, j: (0, c))
        p2_spec = pl.BlockSpec((1, D_MODEL), lambda s, c, j: (0, c))
    else:
        p1_spec = pl.BlockSpec(p1.shape, lambda s, c, j: (0, 0, 0))
        p2_spec = pl.BlockSpec((1, D_MODEL), lambda s, c, j: (0, 0))
    return pl.pallas_call(
        functools.partial(_causal_kernel, mode=mode, S=S, H=H, Tb=Tb, pos0=pos0, ntime=ntime),
        grid=(nseq, ncol, ntime),
        in_specs=[pl.BlockSpec((Tb * S, D_MODEL), lambda s, c, j: (s * ntime + j, c)),
                  pl.BlockSpec((1, H * S, D_MODEL), lambda s, c, j: (s, 0, c)),
                  p1_spec, p2_spec],
        out_specs=pl.BlockSpec((Tb * S, D_MODEL), lambda s, c, j: (s * ntime + j, c)),
        out_shape=jax.ShapeDtypeStruct(x.shape, F32),
        scratch_shapes=[pltpu.VMEM(((H + Tb) * S, D_MODEL), F32)],
        compiler_params=_cparams(("parallel", "parallel", "arbitrary"), VMEM_LIMIT_BYTES),
        name="causal_" + mode,
    )(x, st, p1, p2)


def _causal_group(x, hist, p1, p2, *, mode, B, L, pos0, n_hist):
    C = x.shape[1]
    if L % 256 == 0:
        H = POOL_HIST if mode == "pool" else SUBLANES
        st = jnp.pad(hist, ((0, 0), (H - n_hist, 0), (0, 0)))
        return _causal(x, st, p1, p2, mode=mode, nseq=B, S=1, H=H, Tb=256, L=L, pos0=pos0)
    H = POOL_HIST if mode == "pool" else n_hist
    x_tm = x.reshape(B, L, C).transpose(1, 0, 2).reshape(L * B, C)
    st = jnp.pad(hist, ((0, 0), (H - n_hist, 0), (0, 0))).transpose(1, 0, 2).reshape(1, H * B, C)
    y = _causal(x_tm, st, p1, p2, mode=mode, nseq=1, S=B, H=H, Tb=L, L=L, pos0=pos0)
    return y.reshape(L, B, C).transpose(1, 0, 2).reshape(B * L, C)


def _cmlp_kernel(z_ref, ws_ref, bias_ref, lg_ref, lb_ref, *out_refs, nsub, write_v):
    o_ref = out_refs[0]
    for c in range(nsub):
        rows = slice(c * CMLP_CHUNK, (c + 1) * CMLP_CHUNK)
        v = _ln(z_ref[rows, CMLP_WIDTH:2 * CMLP_WIDTH], lg_ref[...], lb_ref[...])
        if write_v:
            out_refs[1][rows, :] = v
        vb = v.astype(BF16)
        for h in range(CMLP_HEADS):
            lanes = slice(h * CMLP_HEAD_DIM, (h + 1) * CMLP_HEAD_DIM)
            mixed = _dot(ws_ref[h], vb[:, lanes]) + bias_ref[:, lanes]
            o_ref[rows, lanes] = z_ref[rows, lanes] * mixed


def _cmlp_core(z, ws_eff, bias_eff, ln_g, ln_b, write_v):
    M = z.shape[0]
    tm = min(512, M)
    assert M % tm == 0 and tm % CMLP_CHUNK == 0
    out_shape = [jax.ShapeDtypeStruct((M, CMLP_WIDTH), F32)]
    out_specs = [pl.BlockSpec((tm, CMLP_WIDTH), lambda i: (i, 0))]
    if write_v:
        out_shape.append(jax.ShapeDtypeStruct((M, CMLP_WIDTH), F32))
        out_specs.append(pl.BlockSpec((tm, CMLP_WIDTH), lambda i: (i, 0)))
    return pl.pallas_call(
        functools.partial(_cmlp_kernel, nsub=tm // CMLP_CHUNK, write_v=write_v),
        grid=(M // tm,),
        in_specs=[pl.BlockSpec((tm, 2 * CMLP_WIDTH), lambda i: (i, 0)),
                  pl.BlockSpec((CMLP_HEADS, CMLP_CHUNK, CMLP_CHUNK), lambda i: (0, 0, 0)),
                  pl.BlockSpec((CMLP_CHUNK, CMLP_WIDTH), lambda i: (0, 0)),
                  pl.BlockSpec((1, CMLP_WIDTH), lambda i: (0, 0)),
                  pl.BlockSpec((1, CMLP_WIDTH), lambda i: (0, 0))],
        out_specs=out_specs,
        out_shape=out_shape,
        compiler_params=_cparams(("parallel",), VMEM_LIMIT_BYTES),
        name="cmlp_core",
    )(z, ws_eff, bias_eff, ln_g.reshape(1, -1), ln_b.reshape(1, -1))


def _cmlp_mix_params(w_s, b_s, L):
    q = min(L, CMLP_CHUNK)
    nrep = CMLP_CHUNK // q
    causal = jnp.tril(jnp.ones((q, q), dtype=bool))
    ws = jnp.where(causal[None], w_s[:, :q, :q], 0.0)
    eye = jnp.eye(nrep, dtype=F32)
    ws_eff = jnp.einsum("ab,hts->hatbs", eye, ws).reshape(CMLP_HEADS, CMLP_CHUNK, CMLP_CHUNK)
    bias = jnp.tile(b_s[:, :q], (1, nrep))
    bias_eff = jnp.repeat(bias.T, CMLP_HEAD_DIM, axis=1)
    return ws_eff.astype(BF16), bias_eff


def _ssd_kernel(*refs, nsub, Lc, carry, chunks_per_seq):
    if carry:
        (z_ref, xa_ref, dtr_ref, tril_ref, rm_ref, dtb_ref, a_ref, ex_ref, ext_ref, dsk_ref, ng_ref,
         y_ref, ho_ref, h_scr, acs_scr, xdt_scr, c_scr, b_scr, y_scr) = refs
        h0_ref = None
    else:
        (z_ref, xa_ref, dtr_ref, tril_ref, rm_ref, dtb_ref, a_ref, ex_ref, ext_ref, dsk_ref, ng_ref, h0_ref,
         y_ref, ho_ref, h_scr, acs_scr, xdt_scr, c_scr, b_scr, y_scr) = refs
    ci = pl.program_id(0)
    sub = pl.program_id(1)
    R = SSD_CHUNK
    GW = SSD_HPG * SSD_HEAD_DIM

    if carry:
        @pl.when(ci % chunks_per_seq == 0)
        def _():
            h_scr[...] = jnp.zeros_like(h_scr)

    @pl.when(sub == 0)
    def _():
        trilf = tril_ref[...]
        dt = jax.nn.softplus(dtr_ref[...] + dtb_ref[...])
        a_cs = _dot_exact(trilf, dt * a_ref[...])
        acs_scr[...] = a_cs
        a_cs_t = a_cs.T
        xs = xa_ref[:, 0:SSD_INNER]
        xdt = xs * _dot_sel(dt, ex_ref[...])
        xdt_scr[...] = xdt
        y_scr[...] = dsk_ref[...] * xs
        for g in range(SSD_GROUPS):
            bm = xa_ref[:, SSD_INNER + g * SSD_STATE:SSD_INNER + (g + 1) * SSD_STATE]
            cm = xa_ref[:, SSD_INNER + SSD_GN + g * SSD_STATE:SSD_INNER + SSD_GN + (g + 1) * SSD_STATE]
            b_scr[:, g * SSD_STATE:(g + 1) * SSD_STATE] = bm
            cmb = cm.astype(BF16)
            c_scr[:, g * SSD_STATE:(g + 1) * SSD_STATE] = cmb
            cb = _dot_nt(cmb, bm.astype(BF16)) * trilf
            for jh in range(SSD_HPG):
                hd = g * SSD_HPG + jh
                lanes = slice(hd * SSD_HEAD_DIM, (hd + 1) * SSD_HEAD_DIM)
                seg = a_cs[:, hd:hd + 1] - a_cs_t[hd:hd + 1, :]
                lmat = (cb * jnp.exp(jnp.minimum(seg, 0.0))).astype(BF16)
                y_scr[:, lanes] += _dot(lmat, xdt[:, lanes].astype(BF16))

    rm = rm_ref[sub]
    a_cs = acs_scr[...]
    a_last = acs_scr[pl.ds(sub * Lc + Lc - 1, 1), :]
    e_in = _dot_sel(jnp.exp(a_cs) * rm, ex_ref[...])
    d_end = jnp.exp(jnp.minimum(a_last - a_cs, 0.0)) * rm
    xw = xdt_scr[...] * _dot_sel(d_end, ex_ref[...])
    cd = _sel_dot(ext_ref[...], jnp.broadcast_to(jnp.exp(a_last), (R, SSD_STATE)).T)
    for g in range(SSD_GROUPS):
        rows = slice(g * GW, (g + 1) * GW)
        if carry:
            hg = h_scr[rows, :]
        else:
            hg = h0_ref[0, rows, :]
        yoff = _dot_nt(c_scr[:, g * SSD_STATE:(g + 1) * SSD_STATE], hg.astype(BF16))
        y_scr[:, rows] += e_in[:, rows] * yoff
        st = _dot(xw[:, rows].T.astype(BF16), b_scr[:, g * SSD_STATE:(g + 1) * SSD_STATE].astype(BF16))
        hn = hg * cd[rows, :] + st
        if carry:
            h_scr[rows, :] = hn
        else:
            ho_ref[0, rows, :] = hn

    if carry:
        @pl.when(ci % chunks_per_seq == chunks_per_seq - 1)
        def _():
            ho_ref[0] = h_scr[...]

    @pl.when(sub == nsub - 1)
    def _():
        z = z_ref[...]
        yg = y_scr[...] * (z * jax.nn.sigmoid(z))
        for g in range(SSD_GROUPS):
            lanes = slice(g * GW, (g + 1) * GW)
            t = yg[:, lanes]
            r = lax.rsqrt(jnp.mean(t * t, axis=-1, keepdims=True) + RMS_EPS)
            y_ref[:, lanes] = t * r * ng_ref[:, lanes]


def _ssd_core(z, xa, dtr, h0, dt_bias, a_log, d_skip, norm_g, B, L):
    M = z.shape[0]
    R = SSD_CHUNK
    carry = L >= R
    if carry:
        assert L % R == 0 and h0 is None
        nsub, Lc, cps = 1, R, L // R
    else:
        assert R % L == 0 and M % R == 0
        nsub, Lc, cps = R // L, L, 1
    nchunk = M // R
    r = jnp.arange(R)
    same = (r[:, None] // Lc) == (r[None, :] // Lc)
    tril = (same & (r[None, :] <= r[:, None])).astype(F32)
    rm = ((r[None, :, None] // Lc) == jnp.arange(nsub)[:, None, None]).astype(F32)
    rm = jnp.broadcast_to(rm, (nsub, R, SSD_STATE))
    hsel = (jnp.arange(SSD_STATE)[:, None] == (jnp.arange(SSD_INNER)[None, :] // SSD_HEAD_DIM)).astype(F32)
    pad = SSD_STATE - SSD_HEADS
    dtb = jnp.pad(dt_bias, (0, pad)).reshape(1, SSD_STATE)
    a = jnp.pad(-jnp.exp(a_log), (0, pad)).reshape(1, SSD_STATE)
    dsk = jnp.repeat(d_skip, SSD_HEAD_DIM).reshape(1, SSD_INNER)
    HS = SSD_HEADS * SSD_HEAD_DIM
    const = lambda *shape: pl.BlockSpec(shape, lambda c, s: (0,) * len(shape))
    in_specs = [pl.BlockSpec((R, SSD_INNER), lambda c, s: (c, 0)),
                pl.BlockSpec((R, SSD_CONV_DIM), lambda c, s: (c, 0)),
                pl.BlockSpec((R, SSD_STATE), lambda c, s: (c, 0)),
                const(R, R), const(nsub, R, SSD_STATE), const(1, SSD_STATE), const(1, SSD_STATE),
                const(SSD_STATE, SSD_INNER), const(SSD_INNER, SSD_STATE), const(1, SSD_INNER),
                const(1, SSD_INNER)]
    args = [z, xa, dtr, tril, rm, dtb, a, hsel.astype(BF16), hsel.T.astype(BF16), dsk,
            norm_g.reshape(1, SSD_INNER)]
    if carry:
        ho_spec = pl.BlockSpec((1, HS, SSD_STATE), lambda c, s: (c // cps, 0, 0))
    else:
        in_specs.append(pl.BlockSpec((1, HS, SSD_STATE), lambda c, s: (c * nsub + s, 0, 0)))
        args.append(h0.reshape(B, HS, SSD_STATE))
        ho_spec = pl.BlockSpec((1, HS, SSD_STATE), lambda c, s: (c * nsub + s, 0, 0))
    return pl.pallas_call(
        functools.partial(_ssd_kernel, nsub=nsub, Lc=Lc, carry=carry, chunks_per_seq=cps),
        grid=(nchunk, nsub),
        in_specs=in_specs,
        out_specs=[pl.BlockSpec((R, SSD_INNER), lambda c, s: (c, 0)), ho_spec],
        out_shape=[jax.ShapeDtypeStruct((M, SSD_INNER), F32),
                   jax.ShapeDtypeStruct((B, HS, SSD_STATE), F32)],
        scratch_shapes=[pltpu.VMEM((HS, SSD_STATE), F32),
                        pltpu.VMEM((R, SSD_STATE), F32),
                        pltpu.VMEM((R, SSD_INNER), F32),
                        pltpu.VMEM((R, SSD_GN), BF16),
                        pltpu.VMEM((R, SSD_GN), F32),
                        pltpu.VMEM((R, SSD_INNER), F32)],
        compiler_params=_cparams(("arbitrary", "arbitrary"), VMEM_LIMIT_BYTES),
        name="ssd_core",
    )(*args)


def _compare_exchange(xs, i, j):
    a, b = xs[i], xs[j]
    xs[i], xs[j] = jnp.maximum(a, b), jnp.minimum(a, b)


def _bitonic_sort_desc(xs):
    n = len(xs)
    k = 2
    while k <= n:
        j = k // 2
        while j >= 1:
            for i in range(n):
                p = i ^ j
                if p > i:
                    if (i & k) == 0:
                        _compare_exchange(xs, i, p)
                    else:
                        _compare_exchange(xs, p, i)
            j //= 2
        k *= 2


def _bitonic_merge_desc(xs):
    j = len(xs) // 2
    while j >= 1:
        for i in range(len(xs)):
            p = i ^ j
            if p > i:
                _compare_exchange(xs, i, p)
        j //= 2


def _top16_of_128(s):
    xs = [s[SUBLANES * i:SUBLANES * (i + 1), :] for i in range(PEER_TOPK)]
    _bitonic_sort_desc(xs)
    dropped = jnp.full(xs[0].shape, -jnp.inf, F32)
    for shift in (4, 2, 1):
        ys = [pltpu.roll(xs[PEER_TOPK - 1 - i], shift, 0) for i in range(PEER_TOPK)]
        lost = functools.reduce(jnp.maximum, [jnp.minimum(a, b) for a, b in zip(xs, ys)])
        dropped = jnp.maximum(jnp.maximum(dropped, pltpu.roll(dropped, shift, 0)), lost)
        xs = [jnp.maximum(a, b) for a, b in zip(xs, ys)]
        _bitonic_merge_desc(xs)
    return xs, dropped


def _peer_route_kernel(x_ref, wq_ref, keys_ref, r2_ref, e2_ref, n1_ref, c1_ref, s_scr, m_scr):
    Tq = x_ref.shape[0]
    NEG = -jnp.inf
    q = _dot(x_ref[...].astype(BF16), wq_ref[...]).astype(BF16)
    for h in range(PEER_HEADS):
        for half in range(2):
            col = (h * 2 + half) * PEER_HALF
            s = _dot_nt(keys_ref[h, half], q[:, col:col + PEER_HALF])
            s_scr[half, h] = s
            top, nxt = _top16_of_128(s)
            for r in range(PEER_TOPK):
                m_scr[half, r, h:h + 1, :] = top[r][0:1, :]
            m_scr[half, PEER_TOPK, h:h + 1, :] = nxt[0:1, :]

    pairs = [(r, c) for r in range(PEER_ROUNDS) for c in range(PEER_ROUNDS)
             if (r + 1) * (c + 1) <= PEER_ROUNDS]
    sums = [m_scr[0, r] + m_scr[1, c] for r, c in pairs]
    cum = jnp.zeros((PEER_HEADS, Tq), F32)
    v16 = jnp.full((PEER_HEADS, Tq), NEG, F32)
    v17 = jnp.full((PEER_HEADS, Tq), NEG, F32)
    for _ in range(PEER_ROUNDS):
        mx = functools.reduce(jnp.maximum, sums)
        hit = [sv == mx for sv in sums]
        new = cum + functools.reduce(lambda p, t: p + t, [jnp.where(hm, 1.0, 0.0) for hm in hit])
        v16 = jnp.where((cum < PEER_TOPK) & (new >= PEER_TOPK), mx, v16)
        v17 = jnp.where((cum < PEER_TOPK + 1) & (new >= PEER_TOPK + 1), mx, v17)
        sums = [jnp.where(hm, NEG, sv) for hm, sv in zip(hit, sums)]
        cum = new
    tau = 0.5 * (v16 + v17)
    m1, m2 = m_scr[0, 0], m_scr[1, 0]
    zsum = jnp.zeros((PEER_HEADS, Tq), F32)
    for r, c in pairs:
        if r < PEER_TOPK and c < PEER_TOPK:
            a1, a2 = m_scr[0, r], m_scr[1, c]
            zsum = zsum + jnp.where(a2 >= tau - a1, jnp.exp(a1 - m1) * jnp.exp(a2 - m2), 0.0)
    zinv = 1.0 / zsum
    for h in range(PEER_HEADS):
        s1, s2 = s_scr[0, h], s_scr[1, h]
        thr = tau[h:h + 1, :] - s1
        rank2 = jnp.zeros((PEER_NKEYS, Tq), F32)
        count1 = jnp.zeros((PEER_NKEYS, Tq), F32)
        for r in range(PEER_TOPK):
            a2 = m_scr[1, r, h:h + 1, :]
            rank2 = jnp.where(a2 > s2, r + 1.0, rank2)
            count1 = jnp.where(a2 >= thr, r + 1.0, count1)
        r2_ref[h] = rank2
        n1_ref[h] = count1
        c1_ref[h] = jnp.exp(s1 - m1[h:h + 1, :]) * zinv[h:h + 1, :]
        e2_ref[h] = jnp.exp(s2 - m2[h:h + 1, :])


def _peer_route(hn, wq, keys):
    T = hn.shape[0]
    Tq = min(PEER_TQ, T)
    assert T % Tq == 0
    shape = (PEER_HEADS, PEER_NKEYS, T)
    ospec = pl.BlockSpec((PEER_HEADS, PEER_NKEYS, Tq), lambda i: (0, 0, i))
    return pl.pallas_call(
        _peer_route_kernel,
        grid=(T // Tq,),
        in_specs=[pl.BlockSpec((Tq, D_MODEL), lambda i: (i, 0)),
                  pl.BlockSpec(wq.shape, lambda i: (0, 0)),
                  pl.BlockSpec(keys.shape, lambda i: (0, 0, 0, 0))],
        out_specs=[ospec] * 4,
        out_shape=[jax.ShapeDtypeStruct(shape, F32)] * 4,
        scratch_shapes=[pltpu.VMEM((2, PEER_HEADS, PEER_NKEYS, Tq), F32),
                        pltpu.VMEM((2, PEER_ROUNDS, PEER_HEADS, Tq), F32)],
        compiler_params=_cparams(("parallel",), VMEM_LIMIT_BYTES),
        name="peer_route",
    )(hn, wq, keys)


def _peer_ffn_kernel(x_ref, u_ref, vt_ref, r2_ref, e2_ref, n1_ref, c1_ref, g_ref, b_ref,
                     o_ref, xb_scr, s_scr, p_scr, acc_scr, r2b_scr, e2b_scr):
    c = pl.program_id(1)
    Tb = x_ref.shape[0]
    ni = PEER_EC // PEER_NKEYS
    assert ni == SUBLANES
    bf16_rows = 2 * SUBLANES

    @pl.when(c == 0)
    def _():
        xb_scr[...] = x_ref[...].astype(BF16)
        acc_scr[...] = jnp.zeros_like(acc_scr)
        for h in range(PEER_HEADS):
            r2b_scr[h] = r2_ref[h].astype(BF16)
            e2b_scr[h] = e2_ref[h].astype(BF16)

    s_scr[...] = _dot_nt(u_ref[...], xb_scr[...])

    irows = pl.ds(pl.multiple_of(c * ni, ni), ni)

    def row_tile(rows, ii):
        r = jnp.broadcast_to(rows[ii:ii + 1, :], (bf16_rows, LANES)).astype(BF16)
        return jnp.tile(r, (PEER_NKEYS // bf16_rows, 1))

    for tt in range(Tb // LANES):
        lanes = slice(tt * LANES, (tt + 1) * LANES)
        n1 = [n1_ref[h, irows, lanes] for h in range(PEER_HEADS)]
        c1 = [c1_ref[h, irows, lanes] for h in range(PEER_HEADS)]
        for ii in range(ni):
            rows = slice(ii * PEER_NKEYS, (ii + 1) * PEER_NKEYS)
            gate = jnp.zeros((PEER_NKEYS, LANES), BF16)
            for h in range(PEER_HEADS):
                sel = r2b_scr[h, :, lanes] < row_tile(n1[h], ii)
                gate = gate + jnp.where(sel, e2b_scr[h, :, lanes] * row_tile(c1[h], ii), 0)
            p_scr[rows, lanes] = _gelu(s_scr[rows, lanes]).astype(BF16) * gate
    acc_scr[...] += _dot(vt_ref[...], p_scr[...])

    @pl.when(c == pl.num_programs(1) - 1)
    def _():
        o_ref[...] = _ln(ALPHA * x_ref[...] + acc_scr[...].T, g_ref[...], b_ref[...])


def _peer_ffn(hn, u, vt, r2, e2, n1, c1, g, b):
    T = hn.shape[0]
    E = u.shape[0]
    Tb = min(PEER_TB, T)
    assert T % Tb == 0 and E % PEER_EC == 0
    rspec = pl.BlockSpec((PEER_HEADS, PEER_NKEYS, Tb), lambda i, c: (0, 0, i))
    return pl.pallas_call(
        _peer_ffn_kernel,
        grid=(T // Tb, E // PEER_EC),
        in_specs=[pl.BlockSpec((Tb, D_MODEL), lambda i, c: (i, 0)),
                  pl.BlockSpec((PEER_EC, D_MODEL), lambda i, c: (c, 0)),
                  pl.BlockSpec((D_MODEL, PEER_EC), lambda i, c: (0, c)),
                  rspec, rspec, rspec, rspec,
                  pl.BlockSpec((1, D_MODEL), lambda i, c: (0, 0)),
                  pl.BlockSpec((1, D_MODEL), lambda i, c: (0, 0))],
        out_specs=pl.BlockSpec((Tb, D_MODEL), lambda i, c: (i, 0)),
        out_shape=jax.ShapeDtypeStruct((T, D_MODEL), F32),
        scratch_shapes=[pltpu.VMEM((Tb, D_MODEL), BF16),
                        pltpu.VMEM((PEER_EC, Tb), F32),
                        pltpu.VMEM((PEER_EC, Tb), BF16),
                        pltpu.VMEM((D_MODEL, Tb), F32),
                        pltpu.VMEM((PEER_HEADS, PEER_NKEYS, Tb), BF16),
                        pltpu.VMEM((PEER_HEADS, PEER_NKEYS, Tb), BF16)],
        compiler_params=_cparams(("parallel", "arbitrary"), VMEM_LIMIT_BYTES),
        name="peer_ffn",
    )(hn, u, vt, r2, e2, n1, c1, g.reshape(1, -1), b.reshape(1, -1))


def _trunk(x, pos0, s5_re, s5_im, pool_buf, conv_buf, ssd_h, W, need_v):
    B, L, _ = x.shape
    T = B * L
    h = x.reshape(T, D_MODEL)
    to_tm = lambda a: a.reshape(B, L, -1).transpose(1, 0, 2).reshape(T, -1)
    to_bm = lambda a: a.reshape(L, B, -1).transpose(1, 0, 2).reshape(T, -1)
    cmlp_v = None
    for i in range(DEPTH):
        kind = i % 4
        if kind == 0:
            u = _mm(h, W["s5_w_in"])
            g_tm, hl = _s5_core(to_tm(u), _s5_state_to_blocks(s5_re, s5_im), W["s5_wbb"], W["s5_wc"],
                                W["s5_a"], W["s5_d"], B, L)
            s5_re, s5_im = _s5_blocks_to_state(hl)
            g = to_bm(g_tm)
            mid = _mm(g, W["s5_w_glu"], bias=W["s5_b_glu"], gate=g)
            w_out = W["s5_w_out"]
        elif kind == 1:
            u = _mm(h, W["pool_w_in"])
            mid = _causal_group(u, pool_buf, W["pool_w_grp"], W["pool_scale"].reshape(1, -1),
                                mode="pool", B=B, L=L, pos0=pos0, n_hist=POOL_BUF)
            pool_buf = jnp.concatenate([pool_buf, u.reshape(B, L, -1)], axis=1)[:, -POOL_BUF:]
            w_out = W["pool_w_out"]
        elif kind == 2:
            z = _mm(h, W["cmlp_w_in"], bias=W["cmlp_b_in"], act="gelu")
            ws_eff, bias_eff = _cmlp_mix_params(W["cmlp_w_s"], W["cmlp_b_s"], L)
            outs = _cmlp_core(z, ws_eff, bias_eff, W["cmlp_ln_g"], W["cmlp_ln_b"], write_v=need_v)
            mid = outs[0]
            if need_v:
                cmlp_v = outs[1].reshape(B, L, CMLP_WIDTH)
            w_out = W["cmlp_w_out"]
        else:
            zg = _mm(h, W["ssd_w_z"])
            xbc = _mm(h, W["ssd_w_xbc"])
            dtr = _mm(h, W["ssd_w_dt"])
            xa = _causal_group(xbc, conv_buf, W["ssd_conv_w"], W["ssd_conv_b"].reshape(1, -1),
                               mode="conv", B=B, L=L, pos0=pos0, n_hist=SSD_CONV - 1)
            conv_buf = jnp.concatenate([conv_buf, xbc.reshape(B, L, -1)], axis=1)[:, -(SSD_CONV - 1):]
            mid, hs = _ssd_core(zg, xa, dtr, ssd_h, W["ssd_dt_bias"], W["ssd_a_log"], W["ssd_d"],
                                W["ssd_norm_g"], B, L)
            ssd_h = hs.reshape(B, SSD_HEADS, SSD_HEAD_DIM, SSD_STATE)
            w_out = W["ssd_w_out"]
        h = _mm_ln(mid, w_out, h, W["ln1_g"][i], W["ln1_b"][i])
        r2, e2, n1, c1 = _peer_route(h, W["peer_w_q"][i], W["peer_keys"][i])
        h = _peer_ffn(h, W["peer_u"][i], W["peer_vt"][i], r2, e2, n1, c1, W["ln2_g"][i], W["ln2_b"][i])
    return h.reshape(B, L, D_MODEL), s5_re, s5_im, pool_buf, cmlp_v, conv_buf, ssd_h


def kernel(x_prompt, x_sample, state_s5_re, state_s5_im, state_pool, state_ssd_conv, state_ssd, s5_w_in, s5_a_re, s5_a_im, s5_log_dt, s5_b_re, s5_b_im, s5_c_re, s5_c_im, s5_d, s5_w_glu, s5_b_glu, s5_w_out, pool_w_in, pool_w_grp, pool_scale, pool_w_out, cmlp_w_in, cmlp_b_in, cmlp_ln_g, cmlp_ln_b, cmlp_w_s, cmlp_b_s, cmlp_w_out, ssd_w_in, ssd_conv_w, ssd_conv_b, ssd_dt_bias, ssd_a_log, ssd_d, ssd_norm_g, ssd_w_out, ln1_g, ln1_b, ln2_g, ln2_b, peer_w_q, peer_keys, peer_u, peer_v):
    bf = lambda a: a.astype(BF16)
    wbb, wc, a_blk = _s5_params(s5_a_re, s5_a_im, s5_log_dt, s5_b_re, s5_b_im, s5_c_re, s5_c_im)
    dt_cols = SSD_STATE - SSD_HEADS
    W = dict(
        s5_w_in=bf(s5_w_in), s5_wbb=wbb, s5_wc=wc, s5_a=a_blk, s5_d=s5_d,
        s5_w_glu=bf(s5_w_glu), s5_b_glu=s5_b_glu, s5_w_out=bf(s5_w_out),
        pool_w_in=bf(pool_w_in), pool_w_grp=bf(pool_w_grp), pool_scale=pool_scale, pool_w_out=bf(pool_w_out),
        cmlp_w_in=bf(cmlp_w_in), cmlp_b_in=cmlp_b_in, cmlp_ln_g=cmlp_ln_g, cmlp_ln_b=cmlp_ln_b,
        cmlp_w_s=cmlp_w_s, cmlp_b_s=cmlp_b_s, cmlp_w_out=bf(cmlp_w_out),
        ssd_w_z=bf(ssd_w_in[:, :SSD_INNER]),
        ssd_w_xbc=bf(ssd_w_in[:, SSD_INNER:SSD_INNER + SSD_CONV_DIM]),
        ssd_w_dt=bf(jnp.pad(ssd_w_in[:, SSD_INNER + SSD_CONV_DIM:], ((0, 0), (0, dt_cols)))),
        ssd_conv_w=ssd_conv_w, ssd_conv_b=ssd_conv_b, ssd_dt_bias=ssd_dt_bias, ssd_a_log=ssd_a_log,
        ssd_d=ssd_d, ssd_norm_g=ssd_norm_g, ssd_w_out=bf(ssd_w_out),
        ln1_g=ln1_g, ln1_b=ln1_b, ln2_g=ln2_g, ln2_b=ln2_b,
        peer_w_q=bf(peer_w_q), peer_keys=bf(peer_keys), peer_u=bf(peer_u),
        peer_vt=bf(peer_v).transpose(0, 2, 1),
    )
    bp = x_prompt.shape[0]
    zeros = lambda *s: jnp.zeros(s, F32)
    (y_p, s5_re_p, s5_im_p, pool_p, _, conv_p, ssd_p) = _trunk(
        x_prompt, 0, zeros(bp, S5_GROUPS, S5_STATE), zeros(bp, S5_GROUPS, S5_STATE),
        zeros(bp, POOL_BUF, D_MODEL), zeros(bp, SSD_CONV - 1, SSD_CONV_DIM), None, W, False)
    past_len = 16384
    (y_s, s5_re_s, s5_im_s, pool_s, cmlp_v_s, conv_s, ssd_s) = _trunk(
        x_sample, past_len, state_s5_re, state_s5_im, state_pool, state_ssd_conv, state_ssd, W, True)
    return (y_p, y_s, s5_re_p, s5_im_p, pool_p, conv_p, ssd_p,
            s5_re_s, s5_im_s, pool_s, cmlp_v_s, conv_s, ssd_s)
```

```python
import functools
import math

import jax
import jax.numpy as jnp
from jax import lax
from jax.experimental import pallas as pl
from jax.experimental.pallas import tpu as pltpu

F32 = jnp.float32
BF16 = jnp.bfloat16

D_MODEL = 1024
DEPTH = 4
ALPHA = (2 * DEPTH) ** 0.25
LN_EPS = 1e-5
RMS_EPS = 1e-5

S5_GROUPS = 64
S5_GROUP = 16
S5_STATE = 64
S5_GB = 8
S5_ROWS = 1024

POOL_WINDOWS = (2, 4, 8, 16)
POOL_GROUP = 256
POOL_BUF = 15
POOL_HIST = 16

CMLP_WIDTH = 1024
CMLP_CHUNK = 128
CMLP_HEADS = 4
CMLP_HEAD_DIM = 256

SSD_INNER = 2048
SSD_HEAD_DIM = 64
SSD_HEADS = 32
SSD_STATE = 128
SSD_GROUPS = 4
SSD_HPG = SSD_HEADS // SSD_GROUPS
SSD_CONV = 4
SSD_CHUNK = 128
SSD_CONV_DIM = SSD_INNER + 2 * SSD_GROUPS * SSD_STATE
SSD_GN = SSD_GROUPS * SSD_STATE

PEER_HEADS = 8
PEER_NKEYS = 128
PEER_TOPK = 16
PEER_HALF = 128
PEER_ROUNDS = PEER_TOPK + 1
PEER_TQ = 256
PEER_TB = 1024
PEER_EC = 1024

GELU_K0 = -2.0 * math.sqrt(2.0 / math.pi) * math.log2(math.e)
GELU_K1 = 0.044715 * GELU_K0

LANES = 128
SUBLANES = 8
VMEM_LIMIT_BYTES = 56 * 1024 * 1024


def _cparams(semantics, vmem=None):
    return pltpu.CompilerParams(dimension_semantics=semantics, vmem_limit_bytes=vmem)


def _ln(y, g, b):
    mu = jnp.mean(y, axis=-1, keepdims=True)
    yc = y - mu
    var = jnp.mean(yc * yc, axis=-1, keepdims=True)
    return yc * lax.rsqrt(var + LN_EPS) * g + b


def _gelu(x):
    w = x * (GELU_K0 + GELU_K1 * (x * x))
    return x / (1.0 + jnp.exp2(w))


def _dot(a, b):
    return jnp.dot(a, b, preferred_element_type=F32)


def _dot_nt(a, b):
    return lax.dot_general(a, b, (((1,), (1,)), ((), ())), preferred_element_type=F32)


def _dot_exact(a, b):
    return jnp.dot(a, b, preferred_element_type=F32, precision=lax.Precision.HIGHEST)


def _split_bf16(a):
    hi = a.astype(BF16)
    return hi, (a - hi.astype(F32)).astype(BF16)


def _dot_sel(a, sel):
    hi, lo = _split_bf16(a)
    return _dot(hi, sel) + _dot(lo, sel)


def _sel_dot(sel, b):
    hi, lo = _split_bf16(b)
    return _dot(sel, hi) + _dot(sel, lo)


def _mm_kernel(*refs, act, has_bias, has_gate):
    x_ref, w_ref = refs[0], refs[1]
    k = 2
    acc = _dot(x_ref[...].astype(BF16), w_ref[...])
    if has_bias:
        acc = acc + refs[k][...]
        k += 1
    if act == "gelu":
        acc = _gelu(acc)
    if has_gate:
        acc = refs[k][...] * jax.nn.sigmoid(acc)
        k += 1
    refs[k][...] = acc


def _mm(x, w, bias=None, act=None, gate=None, tm=1024, tn=1024):
    M, K = x.shape
    N = w.shape[1]
    tm, tn = min(tm, M), min(tn, N)
    assert M % tm == 0 and N % tn == 0
    in_specs = [pl.BlockSpec((tm, K), lambda i, j: (i, 0)),
                pl.BlockSpec((K, tn), lambda i, j: (0, j))]
    args = [x, w]
    if bias is not None:
        in_specs.append(pl.BlockSpec((1, tn), lambda i, j: (0, j)))
        args.append(bias.reshape(1, N))
    if gate is not None:
        in_specs.append(pl.BlockSpec((tm, tn), lambda i, j: (i, j)))
        args.append(gate)
    return pl.pallas_call(
        functools.partial(_mm_kernel, act=act, has_bias=bias is not None, has_gate=gate is not None),
        grid=(M // tm, N // tn),
        in_specs=in_specs,
        out_specs=pl.BlockSpec((tm, tn), lambda i, j: (i, j)),
        out_shape=jax.ShapeDtypeStruct((M, N), F32),
        compiler_params=_cparams(("parallel", "parallel"), VMEM_LIMIT_BYTES),
        name="mm",
    )(*args)


def _mm_ln_kernel(x_ref, w_ref, h_ref, g_ref, b_ref, o_ref):
    acc = _dot(x_ref[...].astype(BF16), w_ref[...])
    o_ref[...] = _ln(ALPHA * h_ref[...] + acc, g_ref[...], b_ref[...])


def _mm_ln(x, w, h, g, b, tm=256):
    M, K = x.shape
    N = w.shape[1]
    tm = min(tm, M)
    assert M % tm == 0
    return pl.pallas_call(
        _mm_ln_kernel,
        grid=(M // tm,),
        in_specs=[pl.BlockSpec((tm, K), lambda i: (i, 0)),
                  pl.BlockSpec((K, N), lambda i: (0, 0)),
                  pl.BlockSpec((tm, N), lambda i: (i, 0)),
                  pl.BlockSpec((1, N), lambda i: (0, 0)),
                  pl.BlockSpec((1, N), lambda i: (0, 0))],
        out_specs=pl.BlockSpec((tm, N), lambda i: (i, 0)),
        out_shape=jax.ShapeDtypeStruct((M, N), F32),
        compiler_params=_cparams(("parallel",), VMEM_LIMIT_BYTES),
        name="mm_ln",
    )(x, w, h, g.reshape(1, N), b.reshape(1, N))


def _s5_kernel(u_ref, wbb_ref, wc_ref, a_ref, d_ref, h0_ref, g_ref, hl_ref,
               h_scr, bu_scr, hs_scr, *, B, Tc):
    tc = pl.program_id(1)
    P = S5_GB * S5_STATE

    @pl.when(tc == 0)
    def _():
        h_scr[...] = h0_ref[0]

    u = u_ref[...]
    bu_scr[...] = _dot(u.astype(BF16), wbb_ref[0])
    ar = jnp.broadcast_to(a_ref[0, 0:1, :], (SUBLANES, P))
    ai = jnp.broadcast_to(a_ref[0, 1:2, :], (SUBLANES, P))

    def seq_body(sb, carry):
        r0 = pl.multiple_of(sb * SUBLANES, SUBLANES)

        def t_body(t, h):
            hr, hi = h
            rows = pl.ds(pl.multiple_of(t * B + r0, SUBLANES), SUBLANES)
            nr = ar * hr - ai * hi + bu_scr[rows, 0:P]
            ni = ar * hi + ai * hr + bu_scr[rows, P:2 * P]
            hs_scr[rows, 0:P] = nr
            hs_scr[rows, P:2 * P] = ni
            return nr, ni

        hr, hi = lax.fori_loop(0, Tc, t_body,
                               (h_scr[pl.ds(r0, SUBLANES), 0:P], h_scr[pl.ds(r0, SUBLANES), P:2 * P]))
        h_scr[pl.ds(r0, SUBLANES), 0:P] = hr
        h_scr[pl.ds(r0, SUBLANES), P:2 * P] = hi
        return carry

    lax.fori_loop(0, B // SUBLANES, seq_body, 0)
    y = _dot(hs_scr[...].astype(BF16), wc_ref[0]) + d_ref[...] * u
    g_ref[...] = _gelu(y)

    @pl.when(tc == pl.num_programs(1) - 1)
    def _():
        hl_ref[0] = h_scr[...]


def _s5_core(u_tm, h0_blk, wbb, wc, a_blk, d_skip, B, L):
    Tc = S5_ROWS // B
    assert B % SUBLANES == 0 and L % Tc == 0
    P2 = 2 * S5_GB * S5_STATE
    W = S5_GB * S5_GROUP
    n_gb = S5_GROUPS // S5_GB
    return pl.pallas_call(
        functools.partial(_s5_kernel, B=B, Tc=Tc),
        grid=(n_gb, L // Tc),
        in_specs=[pl.BlockSpec((S5_ROWS, W), lambda g, t: (t, g)),
                  pl.BlockSpec((1, W, P2), lambda g, t: (g, 0, 0)),
                  pl.BlockSpec((1, P2, W), lambda g, t: (g, 0, 0)),
                  pl.BlockSpec((1, 2, P2 // 2), lambda g, t: (g, 0, 0)),
                  pl.BlockSpec((1, W), lambda g, t: (0, g)),
                  pl.BlockSpec((1, B, P2), lambda g, t: (g, 0, 0))],
        out_specs=[pl.BlockSpec((S5_ROWS, W), lambda g, t: (t, g)),
                   pl.BlockSpec((1, B, P2), lambda g, t: (g, 0, 0))],
        out_shape=[jax.ShapeDtypeStruct((L * B, D_MODEL), F32),
                   jax.ShapeDtypeStruct((n_gb, B, P2), F32)],
        scratch_shapes=[pltpu.VMEM((B, P2), F32),
                        pltpu.VMEM((S5_ROWS, P2), F32),
                        pltpu.VMEM((S5_ROWS, P2), F32)],
        compiler_params=_cparams(("parallel", "arbitrary"), VMEM_LIMIT_BYTES),
        name="s5_core",
    )(u_tm, wbb, wc, a_blk, d_skip.reshape(1, D_MODEL), h0_blk)


def _s5_params(a_re, a_im, log_dt, b_re, b_im, c_re, c_im):
    dt = jnp.exp(log_dt)[:, None]
    mag = jnp.exp(a_re * dt)
    lb_r, lb_i = mag * jnp.cos(a_im * dt), mag * jnp.sin(a_im * dt)
    den = a_re * a_re + a_im * a_im
    f_r = ((lb_r - 1.0) * a_re + lb_i * a_im) / den
    f_i = (lb_i * a_re - (lb_r - 1.0) * a_im) / den
    bb_r = f_r[..., None] * b_re - f_i[..., None] * b_im
    bb_i = f_r[..., None] * b_im + f_i[..., None] * b_re
    n_gb = S5_GROUPS // S5_GB
    eye = jnp.eye(S5_GB, dtype=F32)

    def blockdiag_in(bb):
        t = bb.reshape(n_gb, S5_GB, S5_STATE, S5_GROUP).transpose(0, 1, 3, 2)
        return jnp.einsum("bgip,gh->bgihp", t, eye).reshape(n_gb, S5_GB * S5_GROUP, S5_GB * S5_STATE)

    def blockdiag_out(c):
        t = c.reshape(n_gb, S5_GB, S5_GROUP, S5_STATE).transpose(0, 1, 3, 2)
        return jnp.einsum("bgpi,gh->bgphi", t, eye).reshape(n_gb, S5_GB * S5_STATE, S5_GB * S5_GROUP)

    wbb = jnp.concatenate([blockdiag_in(bb_r), blockdiag_in(bb_i)], axis=-1).astype(BF16)
    wc = jnp.concatenate([blockdiag_out(c_re), -blockdiag_out(c_im)], axis=1).astype(BF16)
    a_blk = jnp.stack([lb_r.reshape(n_gb, -1), lb_i.reshape(n_gb, -1)], axis=1)
    return wbb, wc, a_blk


def _s5_state_to_blocks(h_re, h_im):
    B = h_re.shape[0]
    n_gb = S5_GROUPS // S5_GB
    r = h_re.reshape(B, n_gb, -1).transpose(1, 0, 2)
    i = h_im.reshape(B, n_gb, -1).transpose(1, 0, 2)
    return jnp.concatenate([r, i], axis=-1)


def _s5_blocks_to_state(h_blk):
    n_gb, B, P2 = h_blk.shape
    r = h_blk[..., :P2 // 2].transpose(1, 0, 2).reshape(B, S5_GROUPS, S5_STATE)
    i = h_blk[..., P2 // 2:].transpose(1, 0, 2).reshape(B, S5_GROUPS, S5_STATE)
    return r, i


def _causal_kernel(x_ref, st_ref, p1_ref, p2_ref, o_ref, ctx, *, mode, S, H, Tb, pos0, ntime):
    j = pl.program_id(2)
    HS, TS = H * S, Tb * S

    @pl.when(j == 0)
    def _():
        ctx[0:HS, :] = st_ref[0]

    ctx[HS:HS + TS, :] = x_ref[...]

    def back(k, lanes):
        return ctx[HS - k * S:HS - k * S + TS, lanes]

    if mode == "conv":
        acc = p2_ref[...] + back(0, slice(None)) * p1_ref[SSD_CONV - 1:SSD_CONV, :]
        for k in range(1, SSD_CONV):
            acc = acc + back(k, slice(None)) * p1_ref[SSD_CONV - 1 - k:SSD_CONV - k, :]
        o_ref[...] = acc * jax.nn.sigmoid(acc)
    else:
        for gi, w in enumerate(POOL_WINDOWS):
            lanes = slice(gi * POOL_GROUP, (gi + 1) * POOL_GROUP)
            cur = back(0, lanes)
            win = cur
            for k in range(1, w):
                win = win + back(k, lanes)
            if pos0 + 1 >= w:
                mean = win * (1.0 / w)
            else:
                assert S == 1, "position-dependent window counts need step-contiguous rows"
                t = lax.broadcasted_iota(jnp.int32, (TS, POOL_GROUP), 0)
                cnt = jnp.minimum(pos0 + j * Tb + t + 1, w).astype(F32)
                mean = win / cnt
            mixed = _dot((mean - cur).astype(BF16), p1_ref[gi])
            o_ref[:, lanes] = mixed * p2_ref[:, lanes]

    if ntime > 1:
        ctx[0:HS, :] = ctx[TS:TS + HS, :]


def _causal(x, st, p1, p2, *, mode, nseq, S, H, Tb, L, pos0):
    C = x.shape[1]
    ncol = C // D_MODEL
    ntime = L // Tb
    assert L % Tb == 0 and (H * S) % SUBLANES == 0
    if mode == "conv":
        p1_spec = pl.BlockSpec((SSD_CONV, D_MODEL), lambda s, c, j: (0, c))
        p2_spec = pl.BlockSpec((1, D_MODEL), lambda s, c, j: (0, c))
    else:
        p1_spec = pl.BlockSpec(p1.shape, lambda s, c, j: (0, 0, 0))
        p2_spec = pl.BlockSpec((1, D_MODEL), lambda s, c, j: (0, 0))
    return pl.pallas_call(
        functools.partial(_causal_kernel, mode=mode, S=S, H=H, Tb=Tb, pos0=pos0, ntime=ntime),
        grid=(nseq, ncol, ntime),
        in_specs=[pl.BlockSpec((Tb * S, D_MODEL), lambda s, c, j: (s * ntime + j, c)),
                  pl.BlockSpec((1, H * S, D_MODEL), lambda s, c, j: (s, 0, c)),
                  p1_spec, p2_spec],
        out_specs=pl.BlockSpec((Tb * S, D_MODEL), lambda s, c, j: (s * ntime + j, c)),
        out_shape=jax.ShapeDtypeStruct(x.shape, F32),
        scratch_shapes=[pltpu.VMEM(((H + Tb) * S, D_MODEL), F32)],
        compiler_params=_cparams(("parallel", "parallel", "arbitrary"), VMEM_LIMIT_BYTES),
        name="causal_" + mode,
    )(x, st, p1, p2)


def _causal_group(x, hist, p1, p2, *, mode, B, L, pos0, n_hist):
    C = x.shape[1]
    if L % 256 == 0:
        H = POOL_HIST if mode == "pool" else SUBLANES
        st = jnp.pad(hist, ((0, 0), (H - n_hist, 0), (0, 0)))
        return _causal(x, st, p1, p2, mode=mode, nseq=B, S=1, H=H, Tb=256, L=L, pos0=pos0)
    H = POOL_HIST if mode == "pool" else n_hist
    x_tm = x.reshape(B, L, C).transpose(1, 0, 2).reshape(L * B, C)
    st = jnp.pad(hist, ((0, 0), (H - n_hist, 0), (0, 0))).transpose(1, 0, 2).reshape(1, H * B, C)
    y = _causal(x_tm, st, p1, p2, mode=mode, nseq=1, S=B, H=H, Tb=L, L=L, pos0=pos0)
    return y.reshape(L, B, C).transpose(1, 0, 2).reshape(B * L, C)


def _cmlp_kernel(z_ref, ws_ref, bias_ref, lg_ref, lb_ref, *out_refs, nsub, write_v):
    o_ref = out_refs[0]
    for c in range(nsub):
        rows = slice(c * CMLP_CHUNK, (c + 1) * CMLP_CHUNK)
        v = _ln(z_ref[rows, CMLP_WIDTH:2 * CMLP_WIDTH], lg_ref[...], lb_ref[...])
        if write_v:
            out_refs[1][rows, :] = v
        vb = v.astype(BF16)
        for h in range(CMLP_HEADS):
            lanes = slice(h * CMLP_HEAD_DIM, (h + 1) * CMLP_HEAD_DIM)
            mixed = _dot(ws_ref[h], vb[:, lanes]) + bias_ref[:, lanes]
            o_ref[rows, lanes] = z_ref[rows, lanes] * mixed


def _cmlp_core(z, ws_eff, bias_eff, ln_g, ln_b, write_v):
    M = z.shape[0]
    tm = min(512, M)
    assert M % tm == 0 and tm % CMLP_CHUNK == 0
    out_shape = [jax.ShapeDtypeStruct((M, CMLP_WIDTH), F32)]
    out_specs = [pl.BlockSpec((tm, CMLP_WIDTH), lambda i: (i, 0))]
    if write_v:
        out_shape.append(jax.ShapeDtypeStruct((M, CMLP_WIDTH), F32))
        out_specs.append(pl.BlockSpec((tm, CMLP_WIDTH), lambda i: (i, 0)))
    return pl.pallas_call(
        functools.partial(_cmlp_kernel, nsub=tm // CMLP_CHUNK, write_v=write_v),
        grid=(M // tm,),
        in_specs=[pl.BlockSpec((tm, 2 * CMLP_WIDTH), lambda i: (i, 0)),
                  pl.BlockSpec((CMLP_HEADS, CMLP_CHUNK, CMLP_CHUNK), lambda i: (0, 0, 0)),
                  pl.BlockSpec((CMLP_CHUNK, CMLP_WIDTH), lambda i: (0, 0)),
                  pl.BlockSpec((1, CMLP_WIDTH), lambda i: (0, 0)),
                  pl.BlockSpec((1, CMLP_WIDTH), lambda i: (0, 0))],
        out_specs=out_specs,
        out_shape=out_shape,
        compiler_params=_cparams(("parallel",), VMEM_LIMIT_BYTES),
        name="cmlp_core",
    )(z, ws_eff, bias_eff, ln_g.reshape(1, -1), ln_b.reshape(1, -1))


def _cmlp_mix_params(w_s, b_s, L):
    q = min(L, CMLP_CHUNK)
    nrep = CMLP_CHUNK // q
    causal = jnp.tril(jnp.ones((q, q), dtype=bool))
    ws = jnp.where(causal[None], w_s[:, :q, :q], 0.0)
    eye = jnp.eye(nrep, dtype=F32)
    ws_eff = jnp.einsum("ab,hts->hatbs", eye, ws).reshape(CMLP_HEADS, CMLP_CHUNK, CMLP_CHUNK)
    bias = jnp.tile(b_s[:, :q], (1, nrep))
    bias_eff = jnp.repeat(bias.T, CMLP_HEAD_DIM, axis=1)
    return ws_eff.astype(BF16), bias_eff


def _ssd_kernel(*refs, nsub, Lc, carry, chunks_per_seq):
    if carry:
        (z_ref, xa_ref, dtr_ref, tril_ref, rm_ref, dtb_ref, a_ref, ex_ref, ext_ref, dsk_ref, ng_ref,
         y_ref, ho_ref, h_scr, acs_scr, xdt_scr, c_scr, b_scr, y_scr) = refs
        h0_ref = None
    else:
        (z_ref, xa_ref, dtr_ref, tril_ref, rm_ref, dtb_ref, a_ref, ex_ref, ext_ref, dsk_ref, ng_ref, h0_ref,
         y_ref, ho_ref, h_scr, acs_scr, xdt_scr, c_scr, b_scr, y_scr) = refs
    ci = pl.program_id(0)
    sub = pl.program_id(1)
    R = SSD_CHUNK
    GW = SSD_HPG * SSD_HEAD_DIM

    if carry:
        @pl.when(ci % chunks_per_seq == 0)
        def _():
            h_scr[...] = jnp.zeros_like(h_scr)

    @pl.when(sub == 0)
    def _():
        trilf = tril_ref[...]
        dt = jax.nn.softplus(dtr_ref[...] + dtb_ref[...])
        a_cs = _dot_exact(trilf, dt * a_ref[...])
        acs_scr[...] = a_cs
        a_cs_t = a_cs.T
        xs = xa_ref[:, 0:SSD_INNER]
        xdt = xs * _dot_sel(dt, ex_ref[...])
        xdt_scr[...] = xdt
        y_scr[...] = dsk_ref[...] * xs
        for g in range(SSD_GROUPS):
            bm = xa_ref[:, SSD_INNER + g * SSD_STATE:SSD_INNER + (g + 1) * SSD_STATE]
            cm = xa_ref[:, SSD_INNER + SSD_GN + g * SSD_STATE:SSD_INNER + SSD_GN + (g + 1) * SSD_STATE]
            b_scr[:, g * SSD_STATE:(g + 1) * SSD_STATE] = bm
            cmb = cm.astype(BF16)
            c_scr[:, g * SSD_STATE:(g + 1) * SSD_STATE] = cmb
            cb = _dot_nt(cmb, bm.astype(BF16)) * trilf
            for jh in range(SSD_HPG):
                hd = g * SSD_HPG + jh
                lanes = slice(hd * SSD_HEAD_DIM, (hd + 1) * SSD_HEAD_DIM)
                seg = a_cs[:, hd:hd + 1] - a_cs_t[hd:hd + 1, :]
                lmat = (cb * jnp.exp(jnp.minimum(seg, 0.0))).astype(BF16)
                y_scr[:, lanes] += _dot(lmat, xdt[:, lanes].astype(BF16))

    rm = rm_ref[sub]
    a_cs = acs_scr[...]
    a_last = acs_scr[pl.ds(sub * Lc + Lc - 1, 1), :]
    e_in = _dot_sel(jnp.exp(a_cs) * rm, ex_ref[...])
    d_end = jnp.exp(jnp.minimum(a_last - a_cs, 0.0)) * rm
    xw = xdt_scr[...] * _dot_sel(d_end, ex_ref[...])
    cd = _sel_dot(ext_ref[...], jnp.broadcast_to(jnp.exp(a_last), (R, SSD_STATE)).T)
    for g in range(SSD_GROUPS):
        rows = slice(g * GW, (g + 1) * GW)
        if carry:
            hg = h_scr[rows, :]
        else:
            hg = h0_ref[0, rows, :]
        yoff = _dot_nt(c_scr[:, g * SSD_STATE:(g + 1) * SSD_STATE], hg.astype(BF16))
        y_scr[:, rows] += e_in[:, rows] * yoff
        st = _dot(xw[:, rows].T.astype(BF16), b_scr[:, g * SSD_STATE:(g + 1) * SSD_STATE].astype(BF16))
        hn = hg * cd[rows, :] + st
        if carry:
            h_scr[rows, :] = hn
        else:
            ho_ref[0, rows, :] = hn

    if carry:
        @pl.when(ci % chunks_per_seq == chunks_per_seq - 1)
        def _():
            ho_ref[0] = h_scr[...]

    @pl.when(sub == nsub - 1)
    def _():
        z = z_ref[...]
        yg = y_scr[...] * (z * jax.nn.sigmoid(z))
        for g in range(SSD_GROUPS):
            lanes = slice(g * GW, (g + 1) * GW)
            t = yg[:, lanes]
            r = lax.rsqrt(jnp.mean(t * t, axis=-1, keepdims=True) + RMS_EPS)
            y_ref[:, lanes] = t * r * ng_ref[:, lanes]


def _ssd_core(z, xa, dtr, h0, dt_bias, a_log, d_skip, norm_g, B, L):
    M = z.shape[0]
    R = SSD_CHUNK
    carry = L >= R
    if carry:
        assert L % R == 0 and h0 is None
        nsub, Lc, cps = 1, R, L // R
    else:
        assert R % L == 0 and M % R == 0
        nsub, Lc, cps = R // L, L, 1
    nchunk = M // R
    r = jnp.arange(R)
    same = (r[:, None] // Lc) == (r[None, :] // Lc)
    tril = (same & (r[None, :] <= r[:, None])).astype(F32)
    rm = ((r[None, :, None] // Lc) == jnp.arange(nsub)[:, None, None]).astype(F32)
    rm = jnp.broadcast_to(rm, (nsub, R, SSD_STATE))
    hsel = (jnp.arange(SSD_STATE)[:, None] == (jnp.arange(SSD_INNER)[None, :] // SSD_HEAD_DIM)).astype(F32)
    pad = SSD_STATE - SSD_HEADS
    dtb = jnp.pad(dt_bias, (0, pad)).reshape(1, SSD_STATE)
    a = jnp.pad(-jnp.exp(a_log), (0, pad)).reshape(1, SSD_STATE)
    dsk = jnp.repeat(d_skip, SSD_HEAD_DIM).reshape(1, SSD_INNER)
    HS = SSD_HEADS * SSD_HEAD_DIM
    const = lambda *shape: pl.BlockSpec(shape, lambda c, s: (0,) * len(shape))
    in_specs = [pl.BlockSpec((R, SSD_INNER), lambda c, s: (c, 0)),
                pl.BlockSpec((R, SSD_CONV_DIM), lambda c, s: (c, 0)),
                pl.BlockSpec((R, SSD_STATE), lambda c, s: (c, 0)),
                const(R, R), const(nsub, R, SSD_STATE), const(1, SSD_STATE), const(1, SSD_STATE),
                const(SSD_STATE, SSD_INNER), const(SSD_INNER, SSD_STATE), const(1, SSD_INNER),
                const(1, SSD_INNER)]
    args = [z, xa, dtr, tril, rm, dtb, a, hsel.astype(BF16), hsel.T.astype(BF16), dsk,
            norm_g.reshape(1, SSD_INNER)]
    if carry:
        ho_spec = pl.BlockSpec((1, HS, SSD_STATE), lambda c, s: (c // cps, 0, 0))
    else:
        in_specs.append(pl.BlockSpec((1, HS, SSD_STATE), lambda c, s: (c * nsub + s, 0, 0)))
        args.append(h0.reshape(B, HS, SSD_STATE))
        ho_spec = pl.BlockSpec((1, HS, SSD_STATE), lambda c, s: (c * nsub + s, 0, 0))
    return pl.pallas_call(
        functools.partial(_ssd_kernel, nsub=nsub, Lc=Lc, carry=carry, chunks_per_seq=cps),
        grid=(nchunk, nsub),
        in_specs=in_specs,
        out_specs=[pl.BlockSpec((R, SSD_INNER), lambda c, s: (c, 0)), ho_spec],
        out_shape=[jax.ShapeDtypeStruct((M, SSD_INNER), F32),
                   jax.ShapeDtypeStruct((B, HS, SSD_STATE), F32)],
        scratch_shapes=[pltpu.VMEM((HS, SSD_STATE), F32),
                        pltpu.VMEM((R, SSD_STATE), F32),
                        pltpu.VMEM((R, SSD_INNER), F32),
                        pltpu.VMEM((R, SSD_GN), BF16),
                        pltpu.VMEM((R, SSD_GN), F32),
                        pltpu.VMEM((R, SSD_INNER), F32)],
        compiler_params=_cparams(("arbitrary", "arbitrary"), VMEM_LIMIT_BYTES),
        name="ssd_core",
    )(*args)


def _compare_exchange(xs, i, j):
    a, b = xs[i], xs[j]
    xs[i], xs[j] = jnp.maximum(a, b), jnp.minimum(a, b)


def _bitonic_sort_desc(xs):
    n = len(xs)
    k = 2
    while k <= n:
        j = k // 2
        while j >= 1:
            for i in range(n):
                p = i ^ j
                if p > i:
                    if (i & k) == 0:
                        _compare_exchange(xs, i, p)
                    else:
                        _compare_exchange(xs, p, i)
            j //= 2
        k *= 2


def _bitonic_merge_desc(xs):
    j = len(xs) // 2
    while j >= 1:
        for i in range(len(xs)):
            p = i ^ j
            if p > i:
                _compare_exchange(xs, i, p)
        j //= 2


def _top16_of_128(s):
    xs = [s[SUBLANES * i:SUBLANES * (i + 1), :] for i in range(PEER_TOPK)]
    _bitonic_sort_desc(xs)
    dropped = jnp.full(xs[0].shape, -jnp.inf, F32)
    for shift in (4, 2, 1):
        ys = [pltpu.roll(xs[PEER_TOPK - 1 - i], shift, 0) for i in range(PEER_TOPK)]
        lost = functools.reduce(jnp.maximum, [jnp.minimum(a, b) for a, b in zip(xs, ys)])
        dropped = jnp.maximum(jnp.maximum(dropped, pltpu.roll(dropped, shift, 0)), lost)
        xs = [jnp.maximum(a, b) for a, b in zip(xs, ys)]
        _bitonic_merge_desc(xs)
    return xs, dropped


def _peer_route_kernel(x_ref, wq_ref, keys_ref, r2_ref, e2_ref, n1_ref, c1_ref, s_scr, m_scr):
    Tq = x_ref.shape[0]
    NEG = -jnp.inf
    q = _dot(x_ref[...].astype(BF16), wq_ref[...]).astype(BF16)
    for h in range(PEER_HEADS):
        for half in range(2):
            col = (h * 2 + half) * PEER_HALF
            s = _dot_nt(keys_ref[h, half], q[:, col:col + PEER_HALF])
            s_scr[half, h] = s
            top, nxt = _top16_of_128(s)
            for r in range(PEER_TOPK):
                m_scr[half, r, h:h + 1, :] = top[r][0:1, :]
            m_scr[half, PEER_TOPK, h:h + 1, :] = nxt[0:1, :]

    pairs = [(r, c) for r in range(PEER_ROUNDS) for c in range(PEER_ROUNDS)
             if (r + 1) * (c + 1) <= PEER_ROUNDS]
    sums = [m_scr[0, r] + m_scr[1, c] for r, c in pairs]
    cum = jnp.zeros((PEER_HEADS, Tq), F32)
    v16 = jnp.full((PEER_HEADS, Tq), NEG, F32)
    v17 = jnp.full((PEER_HEADS, Tq), NEG, F32)
    for _ in range(PEER_ROUNDS):
        mx = functools.reduce(jnp.maximum, sums)
        hit = [sv == mx for sv in sums]
        new = cum + functools.reduce(lambda p, t: p + t, [jnp.where(hm, 1.0, 0.0) for hm in hit])
        v16 = jnp.where((cum < PEER_TOPK) & (new >= PEER_TOPK), mx, v16)
        v17 = jnp.where((cum < PEER_TOPK + 1) & (new >= PEER_TOPK + 1), mx, v17)
        sums = [jnp.where(hm, NEG, sv) for hm, sv in zip(hit, sums)]
        cum = new
    tau = 0.5 * (v16 + v17)
    m1, m2 = m_scr[0, 0], m_scr[1, 0]
    zsum = jnp.zeros((PEER_HEADS, Tq), F32)
    for r, c in pairs:
        if r < PEER_TOPK and c < PEER_TOPK:
            a1, a2 = m_scr[0, r], m_scr[1, c]
            zsum = zsum + jnp.where(a2 >= tau - a1, jnp.exp(a1 - m1) * jnp.exp(a2 - m2), 0.0)
    zinv = 1.0 / zsum
    for h in range(PEER_HEADS):
        s1, s2 = s_scr[0, h], s_scr[1, h]
        thr = tau[h:h + 1, :] - s1
        rank2 = jnp.zeros((PEER_NKEYS, Tq), F32)
        count1 = jnp.zeros((PEER_NKEYS, Tq), F32)
        for r in range(PEER_TOPK):
            a2 = m_scr[1, r, h:h + 1, :]
            rank2 = jnp.where(a2 > s2, r + 1.0, rank2)
            count1 = jnp.where(a2 >= thr, r + 1.0, count1)
        r2_ref[h] = rank2
        n1_ref[h] = count1
        c1_ref[h] = jnp.exp(s1 - m1[h:h + 1, :]) * zinv[h:h + 1, :]
        e2_ref[h] = jnp.exp(s2 - m2[h:h + 1, :])


def _peer_route(hn, wq, keys):
    T = hn.shape[0]
    Tq = min(PEER_TQ, T)
    assert T % Tq == 0
    shape = (PEER_HEADS, PEER_NKEYS, T)
    ospec = pl.BlockSpec((PEER_HEADS, PEER_NKEYS, Tq), lambda i: (0, 0, i))
    return pl.pallas_call(
        _peer_route_kernel,
        grid=(T // Tq,),
        in_specs=[pl.BlockSpec((Tq, D_MODEL), lambda i: (i, 0)),
                  pl.BlockSpec(wq.shape, lambda i: (0, 0)),
                  pl.BlockSpec(keys.shape, lambda i: (0, 0, 0, 0))],
        out_specs=[ospec] * 4,
        out_shape=[jax.ShapeDtypeStruct(shape, F32)] * 4,
        scratch_shapes=[pltpu.VMEM((2, PEER_HEADS, PEER_NKEYS, Tq), F32),
                        pltpu.VMEM((2, PEER_ROUNDS, PEER_HEADS, Tq), F32)],
        compiler_params=_cparams(("parallel",), VMEM_LIMIT_BYTES),
        name="peer_route",
    )(hn, wq, keys)


def _peer_ffn_kernel(x_ref, u_ref, vt_ref, r2_ref, e2_ref, n1_ref, c1_ref, g_ref, b_ref,
                     o_ref, xb_scr, s_scr, p_scr, acc_scr, r2b_scr, e2b_scr):
    c = pl.program_id(1)
    Tb = x_ref.shape[0]
    ni = PEER_EC // PEER_NKEYS
    assert ni == SUBLANES
    bf16_rows = 2 * SUBLANES

    @pl.when(c == 0)
    def _():
        xb_scr[...] = x_ref[...].astype(BF16)
        acc_scr[...] = jnp.zeros_like(acc_scr)
        for h in range(PEER_HEADS):
            r2b_scr[h] = r2_ref[h].astype(BF16)
            e2b_scr[h] = e2_ref[h].astype(BF16)

    s_scr[...] = _dot_nt(u_ref[...], xb_scr[...])


    def row_tile(rows, ii):
        r = jnp.broadcast_to(rows[ii:ii + 1, :], (bf16_rows, LANES)).astype(BF16)
        return jnp.tile(r, (PEER_NKEYS // bf16_rows, 1))

    for tt in range(Tb // LANES):
        lanes = slice(tt * LANES, (tt + 1) * LANES)
        n1 = [n1_ref[h, :, lanes] for h in range(PEER_HEADS)]
        c1 = [c1_ref[h, :, lanes] for h in range(PEER_HEADS)]
        for ii in range(ni):
            rows = slice(ii * PEER_NKEYS, (ii + 1) * PEER_NKEYS)
            gate = jnp.zeros((PEER_NKEYS, LANES), BF16)
            for h in range(PEER_HEADS):
                sel = r2b_scr[h, :, lanes] < row_tile(n1[h], ii)
                gate = gate + jnp.where(sel, e2b_scr[h, :, lanes] * row_tile(c1[h], ii), 0)
            p_scr[rows, lanes] = _gelu(s_scr[rows, lanes]).astype(BF16) * gate
    acc_scr[...] += _dot(vt_ref[0], p_scr[...])

    @pl.when(c == pl.num_programs(1) - 1)
    def _():
        o_ref[...] = _ln(ALPHA * x_ref[...] + acc_scr[...].T, g_ref[...], b_ref[...])


def _peer_ffn(hn, u, vt, r2, e2, n1, c1, g, b):
    T = hn.shape[0]
    E = u.shape[0]
    Tb = min(PEER_TB, T)
    assert T % Tb == 0 and vt.shape == (E // PEER_EC, D_MODEL, PEER_EC)
    jspec = pl.BlockSpec((PEER_HEADS, PEER_NKEYS, Tb), lambda i, c: (0, 0, i),
                         pipeline_mode=pl.Buffered(1))
    ispec = pl.BlockSpec((PEER_HEADS, PEER_EC // PEER_NKEYS, Tb), lambda i, c: (0, c, i))
    return pl.pallas_call(
        _peer_ffn_kernel,
        grid=(T // Tb, E // PEER_EC),
        in_specs=[pl.BlockSpec((Tb, D_MODEL), lambda i, c: (i, 0)),
                  pl.BlockSpec((PEER_EC, D_MODEL), lambda i, c: (c, 0)),
                  pl.BlockSpec((1, D_MODEL, PEER_EC), lambda i, c: (c, 0, 0)),
                  jspec, jspec, ispec, ispec,
                  pl.BlockSpec((1, D_MODEL), lambda i, c: (0, 0)),
                  pl.BlockSpec((1, D_MODEL), lambda i, c: (0, 0))],
        out_specs=pl.BlockSpec((Tb, D_MODEL), lambda i, c: (i, 0)),
        out_shape=jax.ShapeDtypeStruct((T, D_MODEL), F32),
        scratch_shapes=[pltpu.VMEM((Tb, D_MODEL), BF16),
                        pltpu.VMEM((PEER_EC, Tb), F32),
                        pltpu.VMEM((PEER_EC, Tb), BF16),
                        pltpu.VMEM((D_MODEL, Tb), F32),
                        pltpu.VMEM((PEER_HEADS, PEER_NKEYS, Tb), BF16),
                        pltpu.VMEM((PEER_HEADS, PEER_NKEYS, Tb), BF16)],
        compiler_params=_cparams(("parallel", "arbitrary"), VMEM_LIMIT_BYTES),
        name="peer_ffn",
    )(hn, u, vt, r2, e2, n1, c1, g.reshape(1, -1), b.reshape(1, -1))


def _trunk(x, pos0, s5_re, s5_im, pool_buf, conv_buf, ssd_h, W, need_v):
    B, L, _ = x.shape
    T = B * L
    h = x.reshape(T, D_MODEL)
    to_tm = lambda a: a.reshape(B, L, -1).transpose(1, 0, 2).reshape(T, -1)
    to_bm = lambda a: a.reshape(L, B, -1).transpose(1, 0, 2).reshape(T, -1)
    cmlp_v = None
    for i in range(DEPTH):
        kind = i % 4
        if kind == 0:
            u = _mm(h, W["s5_w_in"])
            g_tm, hl = _s5_core(to_tm(u), _s5_state_to_blocks(s5_re, s5_im), W["s5_wbb"], W["s5_wc"],
                                W["s5_a"], W["s5_d"], B, L)
            s5_re, s5_im = _s5_blocks_to_state(hl)
            g = to_bm(g_tm)
            mid = _mm(g, W["s5_w_glu"], bias=W["s5_b_glu"], gate=g)
            w_out = W["s5_w_out"]
        elif kind == 1:
            u = _mm(h, W["pool_w_in"])
            mid = _causal_group(u, pool_buf, W["pool_w_grp"], W["pool_scale"].reshape(1, -1),
                                mode="pool", B=B, L=L, pos0=pos0, n_hist=POOL_BUF)
            pool_buf = jnp.concatenate([pool_buf, u.reshape(B, L, -1)], axis=1)[:, -POOL_BUF:]
            w_out = W["pool_w_out"]
        elif kind == 2:
            z = _mm(h, W["cmlp_w_in"], bias=W["cmlp_b_in"], act="gelu")
            ws_eff, bias_eff = _cmlp_mix_params(W["cmlp_w_s"], W["cmlp_b_s"], L)
            outs = _cmlp_core(z, ws_eff, bias_eff, W["cmlp_ln_g"], W["cmlp_ln_b"], write_v=need_v)
            mid = outs[0]
            if need_v:
                cmlp_v = outs[1].reshape(B, L, CMLP_WIDTH)
            w_out = W["cmlp_w_out"]
        else:
            zg = _mm(h, W["ssd_w_z"])
            xbc = _mm(h, W["ssd_w_xbc"])
            dtr = _mm(h, W["ssd_w_dt"])
            xa = _causal_group(xbc, conv_buf, W["ssd_conv_w"], W["ssd_conv_b"].reshape(1, -1),
                               mode="conv", B=B, L=L, pos0=pos0, n_hist=SSD_CONV - 1)
            conv_buf = jnp.concatenate([conv_buf, xbc.reshape(B, L, -1)], axis=1)[:, -(SSD_CONV - 1):]
            mid, hs = _ssd_core(zg, xa, dtr, ssd_h, W["ssd_dt_bias"], W["ssd_a_log"], W["ssd_d"],
                                W["ssd_norm_g"], B, L)
            ssd_h = hs.reshape(B, SSD_HEADS, SSD_HEAD_DIM, SSD_STATE)
            w_out = W["ssd_w_out"]
        h = _mm_ln(mid, w_out, h, W["ln1_g"][i], W["ln1_b"][i])
        r2, e2, n1, c1 = _peer_route(h, W["peer_w_q"][i], W["peer_keys"][i])
        h = _peer_ffn(h, W["peer_u"][i], W["peer_vt"][i], r2, e2, n1, c1, W["ln2_g"][i], W["ln2_b"][i])
    return h.reshape(B, L, D_MODEL), s5_re, s5_im, pool_buf, cmlp_v, conv_buf, ssd_h


def kernel(x_prompt, x_sample, state_s5_re, state_s5_im, state_pool, state_ssd_conv, state_ssd, s5_w_in, s5_a_re, s5_a_im, s5_log_dt, s5_b_re, s5_b_im, s5_c_re, s5_c_im, s5_d, s5_w_glu, s5_b_glu, s5_w_out, pool_w_in, pool_w_grp, pool_scale, pool_w_out, cmlp_w_in, cmlp_b_in, cmlp_ln_g, cmlp_ln_b, cmlp_w_s, cmlp_b_s, cmlp_w_out, ssd_w_in, ssd_conv_w, ssd_conv_b, ssd_dt_bias, ssd_a_log, ssd_d, ssd_norm_g, ssd_w_out, ln1_g, ln1_b, ln2_g, ln2_b, peer_w_q, peer_keys, peer_u, peer_v):
    bf = lambda a: a.astype(BF16)
    wbb, wc, a_blk = _s5_params(s5_a_re, s5_a_im, s5_log_dt, s5_b_re, s5_b_im, s5_c_re, s5_c_im)
    dt_cols = SSD_STATE - SSD_HEADS
    W = dict(
        s5_w_in=bf(s5_w_in), s5_wbb=wbb, s5_wc=wc, s5_a=a_blk, s5_d=s5_d,
        s5_w_glu=bf(s5_w_glu), s5_b_glu=s5_b_glu, s5_w_out=bf(s5_w_out),
        pool_w_in=bf(pool_w_in), pool_w_grp=bf(pool_w_grp), pool_scale=pool_scale, pool_w_out=bf(pool_w_out),
        cmlp_w_in=bf(cmlp_w_in), cmlp_b_in=cmlp_b_in, cmlp_ln_g=cmlp_ln_g, cmlp_ln_b=cmlp_ln_b,
        cmlp_w_s=cmlp_w_s, cmlp_b_s=cmlp_b_s, cmlp_w_out=bf(cmlp_w_out),
        ssd_w_z=bf(ssd_w_in[:, :SSD_INNER]),
        ssd_w_xbc=bf(ssd_w_in[:, SSD_INNER:SSD_INNER + SSD_CONV_DIM]),
        ssd_w_dt=bf(jnp.pad(ssd_w_in[:, SSD_INNER + SSD_CONV_DIM:], ((0, 0), (0, dt_cols)))),
        ssd_conv_w=ssd_conv_w, ssd_conv_b=ssd_conv_b, ssd_dt_bias=ssd_dt_bias, ssd_a_log=ssd_a_log,
        ssd_d=ssd_d, ssd_norm_g=ssd_norm_g, ssd_w_out=bf(ssd_w_out),
        ln1_g=ln1_g, ln1_b=ln1_b, ln2_g=ln2_g, ln2_b=ln2_b,
        peer_w_q=bf(peer_w_q), peer_keys=bf(peer_keys), peer_u=bf(peer_u),
        peer_vt=bf(peer_v).reshape(DEPTH, -1, PEER_EC, D_MODEL).transpose(0, 1, 3, 2),
    )
    bp = x_prompt.shape[0]
    zeros = lambda *s: jnp.zeros(s, F32)
    (y_p, s5_re_p, s5_im_p, pool_p, _, conv_p, ssd_p) = _trunk(
        x_prompt, 0, zeros(bp, S5_GROUPS, S5_STATE), zeros(bp, S5_GROUPS, S5_STATE),
        zeros(bp, POOL_BUF, D_MODEL), zeros(bp, SSD_CONV - 1, SSD_CONV_DIM), None, W, False)
    past_len = 16384
    (y_s, s5_re_s, s5_im_s, pool_s, cmlp_v_s, conv_s, ssd_s) = _trunk(
        x_sample, past_len, state_s5_re, state_s5_im, state_pool, state_ssd_conv, state_ssd, W, True)
    return (y_p, y_s, s5_re_p, s5_im_p, pool_p, conv_p, ssd_p,
            s5_re_s, s5_im_s, pool_s, cmlp_v_s, conv_s, ssd_s)
```

```python
import functools
import math

import jax
import jax.numpy as jnp
from jax import lax
from jax.experimental import pallas as pl
from jax.experimental.pallas import tpu as pltpu

F32 = jnp.float32
BF16 = jnp.bfloat16

D_MODEL = 1024
DEPTH = 4
ALPHA = (2 * DEPTH) ** 0.25
LN_EPS = 1e-5
RMS_EPS = 1e-5

S5_GROUPS = 64
S5_GROUP = 16
S5_STATE = 64
S5_GB = 8
S5_ROWS = 1024

POOL_WINDOWS = (2, 4, 8, 16)
POOL_GROUP = 256
POOL_BUF = 15
POOL_HIST = 16

CMLP_WIDTH = 1024
CMLP_CHUNK = 128
CMLP_HEADS = 4
CMLP_HEAD_DIM = 256

SSD_INNER = 2048
SSD_HEAD_DIM = 64
SSD_HEADS = 32
SSD_STATE = 128
SSD_GROUPS = 4
SSD_HPG = SSD_HEADS // SSD_GROUPS
SSD_CONV = 4
SSD_CHUNK = 128
SSD_CONV_DIM = SSD_INNER + 2 * SSD_GROUPS * SSD_STATE
SSD_GN = SSD_GROUPS * SSD_STATE

PEER_HEADS = 8
PEER_NKEYS = 128
PEER_TOPK = 16
PEER_HALF = 128
PEER_ROUNDS = PEER_TOPK + 1
PEER_TQ = 256
PEER_TB = 512
PEER_EC = 1024

GELU_K0 = -2.0 * math.sqrt(2.0 / math.pi) * math.log2(math.e)
GELU_K1 = 0.044715 * GELU_K0

LANES = 128
SUBLANES = 8
VMEM_LIMIT_BYTES = 56 * 1024 * 1024


def _cparams(semantics, vmem=None):
    return pltpu.CompilerParams(dimension_semantics=semantics, vmem_limit_bytes=vmem)


def _ln(y, g, b):
    mu = jnp.mean(y, axis=-1, keepdims=True)
    yc = y - mu
    var = jnp.mean(yc * yc, axis=-1, keepdims=True)
    return yc * lax.rsqrt(var + LN_EPS) * g + b


def _gelu(x):
    w = x * (GELU_K0 + GELU_K1 * (x * x))
    return x / (1.0 + jnp.exp2(w))


def _dot(a, b):
    return jnp.dot(a, b, preferred_element_type=F32)


def _dot_nt(a, b):
    return lax.dot_general(a, b, (((1,), (1,)), ((), ())), preferred_element_type=F32)


def _dot_exact(a, b):
    return jnp.dot(a, b, preferred_element_type=F32, precision=lax.Precision.HIGHEST)


def _split_bf16(a):
    hi = a.astype(BF16)
    return hi, (a - hi.astype(F32)).astype(BF16)


def _dot_sel(a, sel):
    hi, lo = _split_bf16(a)
    return _dot(hi, sel) + _dot(lo, sel)


def _sel_dot(sel, b):
    hi, lo = _split_bf16(b)
    return _dot(sel, hi) + _dot(sel, lo)


def _mm_kernel(*refs, act, has_bias, has_gate):
    x_ref, w_ref = refs[0], refs[1]
    k = 2
    acc = _dot(x_ref[...].astype(BF16), w_ref[...])
    if has_bias:
        acc = acc + refs[k][...]
        k += 1
    if act == "gelu":
        acc = _gelu(acc)
    if has_gate:
        acc = refs[k][...] * jax.nn.sigmoid(acc)
        k += 1
    refs[k][...] = acc


def _mm(x, w, bias=None, act=None, gate=None, tm=1024, tn=1024):
    M, K = x.shape
    N = w.shape[1]
    tm, tn = min(tm, M), min(tn, N)
    assert M % tm == 0 and N % tn == 0
    in_specs = [pl.BlockSpec((tm, K), lambda i, j: (i, 0)),
                pl.BlockSpec((K, tn), lambda i, j: (0, j))]
    args = [x, w]
    if bias is not None:
        in_specs.append(pl.BlockSpec((1, tn), lambda i, j: (0, j)))
        args.append(bias.reshape(1, N))
    if gate is not None:
        in_specs.append(pl.BlockSpec((tm, tn), lambda i, j: (i, j)))
        args.append(gate)
    return pl.pallas_call(
        functools.partial(_mm_kernel, act=act, has_bias=bias is not None, has_gate=gate is not None),
        grid=(M // tm, N // tn),
        in_specs=in_specs,
        out_specs=pl.BlockSpec((tm, tn), lambda i, j: (i, j)),
        out_shape=jax.ShapeDtypeStruct((M, N), F32),
        compiler_params=_cparams(("parallel", "parallel"), VMEM_LIMIT_BYTES),
        name="mm",
    )(*args)


def _mm_ln_kernel(x_ref, w_ref, h_ref, g_ref, b_ref, o_ref):
    acc = _dot(x_ref[...].astype(BF16), w_ref[...])
    o_ref[...] = _ln(ALPHA * h_ref[...] + acc, g_ref[...], b_ref[...])


def _mm_ln(x, w, h, g, b, tm=512):
    M, K = x.shape
    N = w.shape[1]
    tm = min(tm, M)
    assert M % tm == 0
    return pl.pallas_call(
        _mm_ln_kernel,
        grid=(M // tm,),
        in_specs=[pl.BlockSpec((tm, K), lambda i: (i, 0)),
                  pl.BlockSpec((K, N), lambda i: (0, 0)),
                  pl.BlockSpec((tm, N), lambda i: (i, 0)),
                  pl.BlockSpec((1, N), lambda i: (0, 0)),
                  pl.BlockSpec((1, N), lambda i: (0, 0))],
        out_specs=pl.BlockSpec((tm, N), lambda i: (i, 0)),
        out_shape=jax.ShapeDtypeStruct((M, N), F32),
        compiler_params=_cparams(("parallel",), VMEM_LIMIT_BYTES),
        name="mm_ln",
    )(x, w, h, g.reshape(1, N), b.reshape(1, N))


def _s5_kernel(u_ref, wbb_ref, wc_ref, a_ref, d_ref, h0_ref, g_ref, hl_ref,
               h_scr, bu_scr, hs_scr, *, B, Tc):
    tc = pl.program_id(1)
    P = S5_GB * S5_STATE

    @pl.when(tc == 0)
    def _():
        h_scr[...] = h0_ref[0]

    u = u_ref[...]
    bu_scr[...] = _dot(u.astype(BF16), wbb_ref[0])
    ar = jnp.broadcast_to(a_ref[0, 0:1, :], (SUBLANES, P))
    ai = jnp.broadcast_to(a_ref[0, 1:2, :], (SUBLANES, P))

    def seq_body(sb, carry):
        r0 = pl.multiple_of(sb * SUBLANES, SUBLANES)

        def t_body(t, h):
            hr, hi = h
            rows = pl.ds(pl.multiple_of(t * B + r0, SUBLANES), SUBLANES)
            nr = ar * hr - ai * hi + bu_scr[rows, 0:P]
            ni = ar * hi + ai * hr + bu_scr[rows, P:2 * P]
            hs_scr[rows, 0:P] = nr
            hs_scr[rows, P:2 * P] = ni
            return nr, ni

        hr, hi = lax.fori_loop(0, Tc, t_body,
                               (h_scr[pl.ds(r0, SUBLANES), 0:P], h_scr[pl.ds(r0, SUBLANES), P:2 * P]))
        h_scr[pl.ds(r0, SUBLANES), 0:P] = hr
        h_scr[pl.ds(r0, SUBLANES), P:2 * P] = hi
        return carry

    lax.fori_loop(0, B // SUBLANES, seq_body, 0)
    y = _dot(hs_scr[...].astype(BF16), wc_ref[0]) + d_ref[...] * u
    g_ref[...] = _gelu(y)

    @pl.when(tc == pl.num_programs(1) - 1)
    def _():
        hl_ref[0] = h_scr[...]


def _s5_core(u_tm, h0_blk, wbb, wc, a_blk, d_skip, B, L):
    Tc = S5_ROWS // B
    assert B % SUBLANES == 0 and L % Tc == 0
    P2 = 2 * S5_GB * S5_STATE
    W = S5_GB * S5_GROUP
    n_gb = S5_GROUPS // S5_GB
    return pl.pallas_call(
        functools.partial(_s5_kernel, B=B, Tc=Tc),
        grid=(n_gb, L // Tc),
        in_specs=[pl.BlockSpec((S5_ROWS, W), lambda g, t: (t, g)),
                  pl.BlockSpec((1, W, P2), lambda g, t: (g, 0, 0)),
                  pl.BlockSpec((1, P2, W), lambda g, t: (g, 0, 0)),
                  pl.BlockSpec((1, 2, P2 // 2), lambda g, t: (g, 0, 0)),
                  pl.BlockSpec((1, W), lambda g, t: (0, g)),
                  pl.BlockSpec((1, B, P2), lambda g, t: (g, 0, 0))],
        out_specs=[pl.BlockSpec((S5_ROWS, W), lambda g, t: (t, g)),
                   pl.BlockSpec((1, B, P2), lambda g, t: (g, 0, 0))],
        out_shape=[jax.ShapeDtypeStruct((L * B, D_MODEL), F32),
                   jax.ShapeDtypeStruct((n_gb, B, P2), F32)],
        scratch_shapes=[pltpu.VMEM((B, P2), F32),
                        pltpu.VMEM((S5_ROWS, P2), F32),
                        pltpu.VMEM((S5_ROWS, P2), F32)],
        compiler_params=_cparams(("parallel", "arbitrary"), VMEM_LIMIT_BYTES),
        name="s5_core",
    )(u_tm, wbb, wc, a_blk, d_skip.reshape(1, D_MODEL), h0_blk)


def _s5_params(a_re, a_im, log_dt, b_re, b_im, c_re, c_im):
    dt = jnp.exp(log_dt)[:, None]
    mag = jnp.exp(a_re * dt)
    lb_r, lb_i = mag * jnp.cos(a_im * dt), mag * jnp.sin(a_im * dt)
    den = a_re * a_re + a_im * a_im
    f_r = ((lb_r - 1.0) * a_re + lb_i * a_im) / den
    f_i = (lb_i * a_re - (lb_r - 1.0) * a_im) / den
    bb_r = f_r[..., None] * b_re - f_i[..., None] * b_im
    bb_i = f_r[..., None] * b_im + f_i[..., None] * b_re
    n_gb = S5_GROUPS // S5_GB
    eye = jnp.eye(S5_GB, dtype=F32)

    def blockdiag_in(bb):
        t = bb.reshape(n_gb, S5_GB, S5_STATE, S5_GROUP).transpose(0, 1, 3, 2)
        return jnp.einsum("bgip,gh->bgihp", t, eye).reshape(n_gb, S5_GB * S5_GROUP, S5_GB * S5_STATE)

    def blockdiag_out(c):
        t = c.reshape(n_gb, S5_GB, S5_GROUP, S5_STATE).transpose(0, 1, 3, 2)
        return jnp.einsum("bgpi,gh->bgphi", t, eye).reshape(n_gb, S5_GB * S5_STATE, S5_GB * S5_GROUP)

    wbb = jnp.concatenate([blockdiag_in(bb_r), blockdiag_in(bb_i)], axis=-1).astype(BF16)
    wc = jnp.concatenate([blockdiag_out(c_re), -blockdiag_out(c_im)], axis=1).astype(BF16)
    a_blk = jnp.stack([lb_r.reshape(n_gb, -1), lb_i.reshape(n_gb, -1)], axis=1)
    return wbb, wc, a_blk


def _s5_state_to_blocks(h_re, h_im):
    B = h_re.shape[0]
    n_gb = S5_GROUPS // S5_GB
    r = h_re.reshape(B, n_gb, -1).transpose(1, 0, 2)
    i = h_im.reshape(B, n_gb, -1).transpose(1, 0, 2)
    return jnp.concatenate([r, i], axis=-1)


def _s5_blocks_to_state(h_blk):
    n_gb, B, P2 = h_blk.shape
    r = h_blk[..., :P2 // 2].transpose(1, 0, 2).reshape(B, S5_GROUPS, S5_STATE)
    i = h_blk[..., P2 // 2:].transpose(1, 0, 2).reshape(B, S5_GROUPS, S5_STATE)
    return r, i


def _causal_kernel(x_ref, st_ref, p1_ref, p2_ref, o_ref, ctx, *, mode, S, H, Tb, pos0, ntime):
    j = pl.program_id(2)
    HS, TS = H * S, Tb * S

    @pl.when(j == 0)
    def _():
        ctx[0:HS, :] = st_ref[0]

    ctx[HS:HS + TS, :] = x_ref[...]

    def back(k, lanes):
        return ctx[HS - k * S:HS - k * S + TS, lanes]

    if mode == "conv":
        acc = p2_ref[...] + back(0, slice(None)) * p1_ref[SSD_CONV - 1:SSD_CONV, :]
        for k in range(1, SSD_CONV):
            acc = acc + back(k, slice(None)) * p1_ref[SSD_CONV - 1 - k:SSD_CONV - k, :]
        o_ref[...] = acc * jax.nn.sigmoid(acc)
    else:
        for gi, w in enumerate(POOL_WINDOWS):
            lanes = slice(gi * POOL_GROUP, (gi + 1) * POOL_GROUP)
            cur = back(0, lanes)
            win = cur
            for k in range(1, w):
                win = win + back(k, lanes)
            if pos0 + 1 >= w:
                mean = win * (1.0 / w)
            else:
                assert S == 1, "position-dependent window counts need step-contiguous rows"
                t = lax.broadcasted_iota(jnp.int32, (TS, POOL_GROUP), 0)
                cnt = jnp.minimum(pos0 + j * Tb + t + 1, w).astype(F32)
                mean = win / cnt
            mixed = _dot((mean - cur).astype(BF16), p1_ref[gi])
            o_ref[:, lanes] = mixed * p2_ref[:, lanes]

    if ntime > 1:
        ctx[0:HS, :] = ctx[TS:TS + HS, :]


def _causal(x, st, p1, p2, *, mode, nseq, S, H, Tb, L, pos0):
    C = x.shape[1]
    ncol = C // D_MODEL
    ntime = L // Tb
    assert L % Tb == 0 and (H * S) % SUBLANES == 0
    if mode == "conv":
        p1_spec = pl.BlockSpec((SSD_CONV, D_MODEL), lambda s, c, j: (0, c))
        p2_spec = pl.BlockSpec((1, D_MODEL), lambda s, c, j: (0, c))
    else:
        p1_spec = pl.BlockSpec(p1.shape, lambda s, c, j: (0, 0, 0))
        p2_spec = pl.BlockSpec((1, D_MODEL), lambda s, c, j: (0, 0))
    return pl.pallas_call(
        functools.partial(_causal_kernel, mode=mode, S=S, H=H, Tb=Tb, pos0=pos0, ntime=ntime),
        grid=(nseq, ncol, ntime),
        in_specs=[pl.BlockSpec((Tb * S, D_MODEL), lambda s, c, j: (s * ntime + j, c)),
                  pl.BlockSpec((1, H * S, D_MODEL), lambda s, c, j: (s, 0, c)),
                  p1_spec, p2_spec],
        out_specs=pl.BlockSpec((Tb * S, D_MODEL), lambda s, c, j: (s * ntime + j, c)),
        out_shape=jax.ShapeDtypeStruct(x.shape, F32),
        scratch_shapes=[pltpu.VMEM(((H + Tb) * S, D_MODEL), F32)],
        compiler_params=_cparams(("parallel", "parallel", "arbitrary"), VMEM_LIMIT_BYTES),
        name="causal_" + mode,
    )(x, st, p1, p2)


def _causal_group(x, hist, p1, p2, *, mode, B, L, pos0, n_hist):
    C = x.shape[1]
    if L % 256 == 0:
        H = POOL_HIST if mode == "pool" else SUBLANES
        st = jnp.pad(hist, ((0, 0), (H - n_hist, 0), (0, 0)))
        return _causal(x, st, p1, p2, mode=mode, nseq=B, S=1, H=H, Tb=256, L=L, pos0=pos0)
    H = POOL_HIST if mode == "pool" else n_hist
    x_tm = x.reshape(B, L, C).transpose(1, 0, 2).reshape(L * B, C)
    st = jnp.pad(hist, ((0, 0), (H - n_hist, 0), (0, 0))).transpose(1, 0, 2).reshape(1, H * B, C)
    y = _causal(x_tm, st, p1, p2, mode=mode, nseq=1, S=B, H=H, Tb=L, L=L, pos0=pos0)
    return y.reshape(L, B, C).transpose(1, 0, 2).reshape(B * L, C)


def _cmlp_kernel(z_ref, ws_ref, bias_ref, lg_ref, lb_ref, *out_refs, nsub, write_v):
    o_ref = out_refs[0]
    for c in range(nsub):
        rows = slice(c * CMLP_CHUNK, (c + 1) * CMLP_CHUNK)
        v = _ln(z_ref[rows, CMLP_WIDTH:2 * CMLP_WIDTH], lg_ref[...], lb_ref[...])
        if write_v:
            out_refs[1][rows, :] = v
        vb = v.astype(BF16)
        for h in range(CMLP_HEADS):
            lanes = slice(h * CMLP_HEAD_DIM, (h + 1) * CMLP_HEAD_DIM)
            mixed = _dot(ws_ref[h], vb[:, lanes]) + bias_ref[:, lanes]
            o_ref[rows, lanes] = z_ref[rows, lanes] * mixed


def _cmlp_core(z, ws_eff, bias_eff, ln_g, ln_b, write_v):
    M = z.shape[0]
    tm = min(512, M)
    assert M % tm == 0 and tm % CMLP_CHUNK == 0
    out_shape = [jax.ShapeDtypeStruct((M, CMLP_WIDTH), F32)]
    out_specs = [pl.BlockSpec((tm, CMLP_WIDTH), lambda i: (i, 0))]
    if write_v:
        out_shape.append(jax.ShapeDtypeStruct((M, CMLP_WIDTH), F32))
        out_specs.append(pl.BlockSpec((tm, CMLP_WIDTH), lambda i: (i, 0)))
    return pl.pallas_call(
        functools.partial(_cmlp_kernel, nsub=tm // CMLP_CHUNK, write_v=write_v),
        grid=(M // tm,),
        in_specs=[pl.BlockSpec((tm, 2 * CMLP_WIDTH), lambda i: (i, 0)),
                  pl.BlockSpec((CMLP_HEADS, CMLP_CHUNK, CMLP_CHUNK), lambda i: (0, 0, 0)),
                  pl.BlockSpec((CMLP_CHUNK, CMLP_WIDTH), lambda i: (0, 0)),
                  pl.BlockSpec((1, CMLP_WIDTH), lambda i: (0, 0)),
                  pl.BlockSpec((1, CMLP_WIDTH), lambda i: (0, 0))],
        out_specs=out_specs,
        out_shape=out_shape,
        compiler_params=_cparams(("parallel",), VMEM_LIMIT_BYTES),
        name="cmlp_core",
    )(z, ws_eff, bias_eff, ln_g.reshape(1, -1), ln_b.reshape(1, -1))


def _cmlp_mix_params(w_s, b_s, L):
    q = min(L, CMLP_CHUNK)
    nrep = CMLP_CHUNK // q
    causal = jnp.tril(jnp.ones((q, q), dtype=bool))
    ws = jnp.where(causal[None], w_s[:, :q, :q], 0.0)
    eye = jnp.eye(nrep, dtype=F32)
    ws_eff = jnp.einsum("ab,hts->hatbs", eye, ws).reshape(CMLP_HEADS, CMLP_CHUNK, CMLP_CHUNK)
    bias = jnp.tile(b_s[:, :q], (1, nrep))
    bias_eff = jnp.repeat(bias.T, CMLP_HEAD_DIM, axis=1)
    return ws_eff.astype(BF16), bias_eff


def _ssd_kernel(*refs, nsub, Lc, carry, chunks_per_seq):
    if carry:
        (z_ref, xa_ref, dtr_ref, tril_ref, rm_ref, dtb_ref, a_ref, ex_ref, ext_ref, dsk_ref, ng_ref,
         y_ref, ho_ref, h_scr, acs_scr, xdt_scr, c_scr, b_scr, y_scr) = refs
        h0_ref = None
    else:
        (z_ref, xa_ref, dtr_ref, tril_ref, rm_ref, dtb_ref, a_ref, ex_ref, ext_ref, dsk_ref, ng_ref, h0_ref,
         y_ref, ho_ref, h_scr, acs_scr, xdt_scr, c_scr, b_scr, y_scr) = refs
    ci = pl.program_id(0)
    sub = pl.program_id(1)
    R = SSD_CHUNK
    GW = SSD_HPG * SSD_HEAD_DIM

    if carry:
        @pl.when(ci % chunks_per_seq == 0)
        def _():
            h_scr[...] = jnp.zeros_like(h_scr)

    @pl.when(sub == 0)
    def _():
        trilf = tril_ref[...]
        dt = jax.nn.softplus(dtr_ref[...] + dtb_ref[...])
        a_cs = _dot_exact(trilf, dt * a_ref[...])
        acs_scr[...] = a_cs
        a_cs_t = a_cs.T
        xs = xa_ref[:, 0:SSD_INNER]
        xdt = xs * _dot_sel(dt, ex_ref[...])
        xdt_scr[...] = xdt
        y_scr[...] = dsk_ref[...] * xs
        for g in range(SSD_GROUPS):
            bm = xa_ref[:, SSD_INNER + g * SSD_STATE:SSD_INNER + (g + 1) * SSD_STATE]
            cm = xa_ref[:, SSD_INNER + SSD_GN + g * SSD_STATE:SSD_INNER + SSD_GN + (g + 1) * SSD_STATE]
            b_scr[:, g * SSD_STATE:(g + 1) * SSD_STATE] = bm
            cmb = cm.astype(BF16)
            c_scr[:, g * SSD_STATE:(g + 1) * SSD_STATE] = cmb
            cb = _dot_nt(cmb, bm.astype(BF16)) * trilf
            for jh in range(SSD_HPG):
                hd = g * SSD_HPG + jh
                lanes = slice(hd * SSD_HEAD_DIM, (hd + 1) * SSD_HEAD_DIM)
                seg = a_cs[:, hd:hd + 1] - a_cs_t[hd:hd + 1, :]
                lmat = (cb * jnp.exp(jnp.minimum(seg, 0.0))).astype(BF16)
                y_scr[:, lanes] += _dot(lmat, xdt[:, lanes].astype(BF16))

    rm = rm_ref[sub]
    a_cs = acs_scr[...]
    a_last = acs_scr[pl.ds(sub * Lc + Lc - 1, 1), :]
    e_in = _dot_sel(jnp.exp(a_cs) * rm, ex_ref[...])
    d_end = jnp.exp(jnp.minimum(a_last - a_cs, 0.0)) * rm
    xw = xdt_scr[...] * _dot_sel(d_end, ex_ref[...])
    cd = _sel_dot(ext_ref[...], jnp.broadcast_to(jnp.exp(a_last), (R, SSD_STATE)).T)
    for g in range(SSD_GROUPS):
        rows = slice(g * GW, (g + 1) * GW)
        if carry:
            hg = h_scr[rows, :]
        else:
            hg = h0_ref[0, rows, :]
        yoff = _dot_nt(c_scr[:, g * SSD_STATE:(g + 1) * SSD_STATE], hg.astype(BF16))
        y_scr[:, rows] += e_in[:, rows] * yoff
        st = _dot(xw[:, rows].T.astype(BF16), b_scr[:, g * SSD_STATE:(g + 1) * SSD_STATE].astype(BF16))
        hn = hg * cd[rows, :] + st
        if carry:
            h_scr[rows, :] = hn
        else:
            ho_ref[0, rows, :] = hn

    if carry:
        @pl.when(ci % chunks_per_seq == chunks_per_seq - 1)
        def _():
            ho_ref[0] = h_scr[...]

    @pl.when(sub == nsub - 1)
    def _():
        z = z_ref[...]
        yg = y_scr[...] * (z * jax.nn.sigmoid(z))
        for g in range(SSD_GROUPS):
            lanes = slice(g * GW, (g + 1) * GW)
            t = yg[:, lanes]
            r = lax.rsqrt(jnp.mean(t * t, axis=-1, keepdims=True) + RMS_EPS)
            y_ref[:, lanes] = t * r * ng_ref[:, lanes]


def _ssd_core(z, xa, dtr, h0, dt_bias, a_log, d_skip, norm_g, B, L):
    M = z.shape[0]
    R = SSD_CHUNK
    carry = L >= R
    if carry:
        assert L % R == 0 and h0 is None
        nsub, Lc, cps = 1, R, L // R
    else:
        assert R % L == 0 and M % R == 0
        nsub, Lc, cps = R // L, L, 1
    nchunk = M // R
    r = jnp.arange(R)
    same = (r[:, None] // Lc) == (r[None, :] // Lc)
    tril = (same & (r[None, :] <= r[:, None])).astype(F32)
    rm = ((r[None, :, None] // Lc) == jnp.arange(nsub)[:, None, None]).astype(F32)
    rm = jnp.broadcast_to(rm, (nsub, R, SSD_STATE))
    hsel = (jnp.arange(SSD_STATE)[:, None] == (jnp.arange(SSD_INNER)[None, :] // SSD_HEAD_DIM)).astype(F32)
    pad = SSD_STATE - SSD_HEADS
    dtb = jnp.pad(dt_bias, (0, pad)).reshape(1, SSD_STATE)
    a = jnp.pad(-jnp.exp(a_log), (0, pad)).reshape(1, SSD_STATE)
    dsk = jnp.repeat(d_skip, SSD_HEAD_DIM).reshape(1, SSD_INNER)
    HS = SSD_HEADS * SSD_HEAD_DIM
    const = lambda *shape: pl.BlockSpec(shape, lambda c, s: (0,) * len(shape))
    in_specs = [pl.BlockSpec((R, SSD_INNER), lambda c, s: (c, 0)),
                pl.BlockSpec((R, SSD_CONV_DIM), lambda c, s: (c, 0)),
                pl.BlockSpec((R, SSD_STATE), lambda c, s: (c, 0)),
                const(R, R), const(nsub, R, SSD_STATE), const(1, SSD_STATE), const(1, SSD_STATE),
                const(SSD_STATE, SSD_INNER), const(SSD_INNER, SSD_STATE), const(1, SSD_INNER),
                const(1, SSD_INNER)]
    args = [z, xa, dtr, tril, rm, dtb, a, hsel.astype(BF16), hsel.T.astype(BF16), dsk,
            norm_g.reshape(1, SSD_INNER)]
    if carry:
        ho_spec = pl.BlockSpec((1, HS, SSD_STATE), lambda c, s: (c // cps, 0, 0))
    else:
        in_specs.append(pl.BlockSpec((1, HS, SSD_STATE), lambda c, s: (c * nsub + s, 0, 0)))
        args.append(h0.reshape(B, HS, SSD_STATE))
        ho_spec = pl.BlockSpec((1, HS, SSD_STATE), lambda c, s: (c * nsub + s, 0, 0))
    return pl.pallas_call(
        functools.partial(_ssd_kernel, nsub=nsub, Lc=Lc, carry=carry, chunks_per_seq=cps),
        grid=(nchunk, nsub),
        in_specs=in_specs,
        out_specs=[pl.BlockSpec((R, SSD_INNER), lambda c, s: (c, 0)), ho_spec],
        out_shape=[jax.ShapeDtypeStruct((M, SSD_INNER), F32),
                   jax.ShapeDtypeStruct((B, HS, SSD_STATE), F32)],
        scratch_shapes=[pltpu.VMEM((HS, SSD_STATE), F32),
                        pltpu.VMEM((R, SSD_STATE), F32),
                        pltpu.VMEM((R, SSD_INNER), F32),
                        pltpu.VMEM((R, SSD_GN), BF16),
                        pltpu.VMEM((R, SSD_GN), F32),
                        pltpu.VMEM((R, SSD_INNER), F32)],
        compiler_params=_cparams(("arbitrary", "arbitrary"), VMEM_LIMIT_BYTES),
        name="ssd_core",
    )(*args)


def _compare_exchange(xs, i, j):
    a, b = xs[i], xs[j]
    xs[i], xs[j] = jnp.maximum(a, b), jnp.minimum(a, b)


def _bitonic_sort_desc(xs):
    n = len(xs)
    k = 2
    while k <= n:
        j = k // 2
        while j >= 1:
            for i in range(n):
                p = i ^ j
                if p > i:
                    if (i & k) == 0:
                        _compare_exchange(xs, i, p)
                    else:
                        _compare_exchange(xs, p, i)
            j //= 2
        k *= 2


def _bitonic_merge_desc(xs):
    j = len(xs) // 2
    while j >= 1:
        for i in range(len(xs)):
            p = i ^ j
            if p > i:
                _compare_exchange(xs, i, p)
        j //= 2


def _top16_of_128(s):
    xs = [s[SUBLANES * i:SUBLANES * (i + 1), :] for i in range(PEER_TOPK)]
    _bitonic_sort_desc(xs)
    dropped = jnp.full(xs[0].shape, -jnp.inf, F32)
    for shift in (4, 2, 1):
        ys = [pltpu.roll(xs[PEER_TOPK - 1 - i], shift, 0) for i in range(PEER_TOPK)]
        lost = functools.reduce(jnp.maximum, [jnp.minimum(a, b) for a, b in zip(xs, ys)])
        dropped = jnp.maximum(jnp.maximum(dropped, pltpu.roll(dropped, shift, 0)), lost)
        xs = [jnp.maximum(a, b) for a, b in zip(xs, ys)]
        _bitonic_merge_desc(xs)
    return xs, dropped


def _peer_route_kernel(x_ref, wq_ref, keys_ref, r2_ref, e2_ref, n1_ref, c1_ref, s_scr, m_scr):
    Tq = x_ref.shape[0]
    NEG = -jnp.inf
    q = _dot(x_ref[...].astype(BF16), wq_ref[...]).astype(BF16)
    for h in range(PEER_HEADS):
        for half in range(2):
            col = (h * 2 + half) * PEER_HALF
            s = _dot_nt(keys_ref[h, half], q[:, col:col + PEER_HALF])
            s_scr[half, h] = s
            top, nxt = _top16_of_128(s)
            for r in range(PEER_TOPK):
                m_scr[half, r, h:h + 1, :] = top[r][0:1, :]
            m_scr[half, PEER_TOPK, h:h + 1, :] = nxt[0:1, :]

    pairs = [(r, c) for r in range(PEER_ROUNDS) for c in range(PEER_ROUNDS)
             if (r + 1) * (c + 1) <= PEER_ROUNDS]
    sums = [m_scr[0, r] + m_scr[1, c] for r, c in pairs]
    cum = jnp.zeros((PEER_HEADS, Tq), F32)
    v16 = jnp.full((PEER_HEADS, Tq), NEG, F32)
    v17 = jnp.full((PEER_HEADS, Tq), NEG, F32)
    for _ in range(PEER_ROUNDS):
        mx = functools.reduce(jnp.maximum, sums)
        hit = [sv == mx for sv in sums]
        new = cum + functools.reduce(lambda p, t: p + t, [jnp.where(hm, 1.0, 0.0) for hm in hit])
        v16 = jnp.where((cum < PEER_TOPK) & (new >= PEER_TOPK), mx, v16)
        v17 = jnp.where((cum < PEER_TOPK + 1) & (new >= PEER_TOPK + 1), mx, v17)
        sums = [jnp.where(hm, NEG, sv) for hm, sv in zip(hit, sums)]
        cum = new
    tau = 0.5 * (v16 + v17)
    m1, m2 = m_scr[0, 0], m_scr[1, 0]
    zsum = jnp.zeros((PEER_HEADS, Tq), F32)
    for r, c in pairs:
        if r < PEER_TOPK and c < PEER_TOPK:
            a1, a2 = m_scr[0, r], m_scr[1, c]
            zsum = zsum + jnp.where(a2 >= tau - a1, jnp.exp(a1 - m1) * jnp.exp(a2 - m2), 0.0)
    zinv = 1.0 / zsum
    for h in range(PEER_HEADS):
        s1, s2 = s_scr[0, h], s_scr[1, h]
        thr = tau[h:h + 1, :] - s1
        rank2 = jnp.zeros((PEER_NKEYS, Tq), F32)
        count1 = jnp.zeros((PEER_NKEYS, Tq), F32)
        for r in range(PEER_TOPK):
            a2 = m_scr[1, r, h:h + 1, :]
            rank2 = jnp.where(a2 > s2, r + 1.0, rank2)
            count1 = jnp.where(a2 >= thr, r + 1.0, count1)
        r2_ref[h] = rank2
        n1_ref[h] = count1
        c1_ref[h] = jnp.exp(s1 - m1[h:h + 1, :]) * zinv[h:h + 1, :]
        e2_ref[h] = jnp.exp(s2 - m2[h:h + 1, :])


def _peer_route(hn, wq, keys):
    T = hn.shape[0]
    Tq = min(PEER_TQ, T)
    assert T % Tq == 0
    shape = (PEER_HEADS, PEER_NKEYS, T)
    ospec = pl.BlockSpec((PEER_HEADS, PEER_NKEYS, Tq), lambda i: (0, 0, i))
    return pl.pallas_call(
        _peer_route_kernel,
        grid=(T // Tq,),
        in_specs=[pl.BlockSpec((Tq, D_MODEL), lambda i: (i, 0)),
                  pl.BlockSpec(wq.shape, lambda i: (0, 0)),
                  pl.BlockSpec(keys.shape, lambda i: (0, 0, 0, 0))],
        out_specs=[ospec] * 4,
        out_shape=[jax.ShapeDtypeStruct(shape, F32)] * 4,
        scratch_shapes=[pltpu.VMEM((2, PEER_HEADS, PEER_NKEYS, Tq), F32),
                        pltpu.VMEM((2, PEER_ROUNDS, PEER_HEADS, Tq), F32)],
        compiler_params=_cparams(("parallel",), VMEM_LIMIT_BYTES),
        name="peer_route",
    )(hn, wq, keys)


def _peer_ffn_kernel(x_ref, u_ref, vt_ref, r2_ref, e2_ref, n1_ref, c1_ref, g_ref, b_ref,
                     o_ref, xb_scr, s_scr, p_scr, acc_scr, r2b_scr, e2b_scr):
    c = pl.program_id(1)
    Tb = x_ref.shape[0]
    ni = PEER_EC // PEER_NKEYS
    assert ni == SUBLANES
    bf16_rows = 2 * SUBLANES

    @pl.when(c == 0)
    def _():
        xb_scr[...] = x_ref[...].astype(BF16)
        acc_scr[...] = jnp.zeros_like(acc_scr)
        for h in range(PEER_HEADS):
            r2b_scr[h] = r2_ref[h].astype(BF16)
            e2b_scr[h] = e2_ref[h].astype(BF16)

    s_scr[...] = _dot_nt(u_ref[...], xb_scr[...])

    def row_tile(rows, ii):
        r = jnp.broadcast_to(rows[ii:ii + 1, :], (bf16_rows, LANES)).astype(BF16)
        return jnp.tile(r, (PEER_NKEYS // bf16_rows, 1))

    for tt in range(Tb // LANES):
        lanes = slice(tt * LANES, (tt + 1) * LANES)
        n1 = [n1_ref[h, :, lanes] for h in range(PEER_HEADS)]
        c1 = [c1_ref[h, :, lanes] for h in range(PEER_HEADS)]
        for ii in range(ni):
            rows = slice(ii * PEER_NKEYS, (ii + 1) * PEER_NKEYS)
            gate = jnp.zeros((PEER_NKEYS, LANES), BF16)
            for h in range(PEER_HEADS):
                sel = r2b_scr[h, :, lanes] < row_tile(n1[h], ii)
                gate = gate + jnp.where(sel, e2b_scr[h, :, lanes] * row_tile(c1[h], ii), 0)
            p_scr[rows, lanes] = _gelu(s_scr[rows, lanes]).astype(BF16) * gate
    acc_scr[...] += _dot(vt_ref[0], p_scr[...])

    @pl.when(c == pl.num_programs(1) - 1)
    def _():
        o_ref[...] = _ln(ALPHA * x_ref[...] + acc_scr[...].T, g_ref[...], b_ref[...])


def _peer_ffn(hn, u, vt, r2, e2, n1, c1, g, b):
    T = hn.shape[0]
    E = u.shape[0]
    Tb = min(PEER_TB, T)
    assert T % Tb == 0 and vt.shape == (E // PEER_EC, D_MODEL, PEER_EC)
    jspec = pl.BlockSpec((PEER_HEADS, PEER_NKEYS, Tb), lambda i, c: (0, 0, i))
    ispec = pl.BlockSpec((PEER_HEADS, PEER_EC // PEER_NKEYS, Tb), lambda i, c: (0, c, i))
    return pl.pallas_call(
        _peer_ffn_kernel,
        grid=(T // Tb, E // PEER_EC),
        in_specs=[pl.BlockSpec((Tb, D_MODEL), lambda i, c: (i, 0)),
                  pl.BlockSpec((PEER_EC, D_MODEL), lambda i, c: (c, 0)),
                  pl.BlockSpec((1, D_MODEL, PEER_EC), lambda i, c: (c, 0, 0)),
                  jspec, jspec, ispec, ispec,
                  pl.BlockSpec((1, D_MODEL), lambda i, c: (0, 0)),
                  pl.BlockSpec((1, D_MODEL), lambda i, c: (0, 0))],
        out_specs=pl.BlockSpec((Tb, D_MODEL), lambda i, c: (i, 0)),
        out_shape=jax.ShapeDtypeStruct((T, D_MODEL), F32),
        scratch_shapes=[pltpu.VMEM((Tb, D_MODEL), BF16),
                        pltpu.VMEM((PEER_EC, Tb), F32),
                        pltpu.VMEM((PEER_EC, Tb), BF16),
                        pltpu.VMEM((D_MODEL, Tb), F32),
                        pltpu.VMEM((PEER_HEADS, PEER_NKEYS, Tb), BF16),
                        pltpu.VMEM((PEER_HEADS, PEER_NKEYS, Tb), BF16)],
        compiler_params=_cparams(("parallel", "arbitrary"), VMEM_LIMIT_BYTES),
        name="peer_ffn",
    )(hn, u, vt, r2, e2, n1, c1, g.reshape(1, -1), b.reshape(1, -1))


def _trunk(x, pos0, s5_re, s5_im, pool_buf, conv_buf, ssd_h, W, need_v):
    B, L, _ = x.shape
    T = B * L
    h = x.reshape(T, D_MODEL)
    to_tm = lambda a: a.reshape(B, L, -1).transpose(1, 0, 2).reshape(T, -1)
    to_bm = lambda a: a.reshape(L, B, -1).transpose(1, 0, 2).reshape(T, -1)
    cmlp_v = None
    for i in range(DEPTH):
        kind = i % 4
        if kind == 0:
            u = _mm(h, W["s5_w_in"])
            g_tm, hl = _s5_core(to_tm(u), _s5_state_to_blocks(s5_re, s5_im), W["s5_wbb"], W["s5_wc"],
                                W["s5_a"], W["s5_d"], B, L)
            s5_re, s5_im = _s5_blocks_to_state(hl)
            g = to_bm(g_tm)
            mid = _mm(g, W["s5_w_glu"], bias=W["s5_b_glu"], gate=g)
            w_out = W["s5_w_out"]
        elif kind == 1:
            u = _mm(h, W["pool_w_in"])
            mid = _causal_group(u, pool_buf, W["pool_w_grp"], W["pool_scale"].reshape(1, -1),
                                mode="pool", B=B, L=L, pos0=pos0, n_hist=POOL_BUF)
            pool_buf = jnp.concatenate([pool_buf, u.reshape(B, L, -1)], axis=1)[:, -POOL_BUF:]
            w_out = W["pool_w_out"]
        elif kind == 2:
            z = _mm(h, W["cmlp_w_in"], bias=W["cmlp_b_in"], act="gelu")
            ws_eff, bias_eff = _cmlp_mix_params(W["cmlp_w_s"], W["cmlp_b_s"], L)
            outs = _cmlp_core(z, ws_eff, bias_eff, W["cmlp_ln_g"], W["cmlp_ln_b"], write_v=need_v)
            mid = outs[0]
            if need_v:
                cmlp_v = outs[1].reshape(B, L, CMLP_WIDTH)
            w_out = W["cmlp_w_out"]
        else:
            zg = _mm(h, W["ssd_w_z"])
            xbc = _mm(h, W["ssd_w_xbc"])
            dtr = _mm(h, W["ssd_w_dt"])
            xa = _causal_group(xbc, conv_buf, W["ssd_conv_w"], W["ssd_conv_b"].reshape(1, -1),
                               mode="conv", B=B, L=L, pos0=pos0, n_hist=SSD_CONV - 1)
            conv_buf = jnp.concatenate([conv_buf, xbc.reshape(B, L, -1)], axis=1)[:, -(SSD_CONV - 1):]
            mid, hs = _ssd_core(zg, xa, dtr, ssd_h, W["ssd_dt_bias"], W["ssd_a_log"], W["ssd_d"],
                                W["ssd_norm_g"], B, L)
            ssd_h = hs.reshape(B, SSD_HEADS, SSD_HEAD_DIM, SSD_STATE)
            w_out = W["ssd_w_out"]
        h = _mm_ln(mid, w_out, h, W["ln1_g"][i], W["ln1_b"][i])
        r2, e2, n1, c1 = _peer_route(h, W["peer_w_q"][i], W["peer_keys"][i])
        h = _peer_ffn(h, W["peer_u"][i], W["peer_vt"][i], r2, e2, n1, c1, W["ln2_g"][i], W["ln2_b"][i])
    return h.reshape(B, L, D_MODEL), s5_re, s5_im, pool_buf, cmlp_v, conv_buf, ssd_h


def kernel(x_prompt, x_sample, state_s5_re, state_s5_im, state_pool, state_ssd_conv, state_ssd, s5_w_in, s5_a_re, s5_a_im, s5_log_dt, s5_b_re, s5_b_im, s5_c_re, s5_c_im, s5_d, s5_w_glu, s5_b_glu, s5_w_out, pool_w_in, pool_w_grp, pool_scale, pool_w_out, cmlp_w_in, cmlp_b_in, cmlp_ln_g, cmlp_ln_b, cmlp_w_s, cmlp_b_s, cmlp_w_out, ssd_w_in, ssd_conv_w, ssd_conv_b, ssd_dt_bias, ssd_a_log, ssd_d, ssd_norm_g, ssd_w_out, ln1_g, ln1_b, ln2_g, ln2_b, peer_w_q, peer_keys, peer_u, peer_v):
    bf = lambda a: a.astype(BF16)
    wbb, wc, a_blk = _s5_params(s5_a_re, s5_a_im, s5_log_dt, s5_b_re, s5_b_im, s5_c_re, s5_c_im)
    dt_cols = SSD_STATE - SSD_HEADS
    W = dict(
        s5_w_in=bf(s5_w_in), s5_wbb=wbb, s5_wc=wc, s5_a=a_blk, s5_d=s5_d,
        s5_w_glu=bf(s5_w_glu), s5_b_glu=s5_b_glu, s5_w_out=bf(s5_w_out),
        pool_w_in=bf(pool_w_in), pool_w_grp=bf(pool_w_grp), pool_scale=pool_scale, pool_w_out=bf(pool_w_out),
        cmlp_w_in=bf(cmlp_w_in), cmlp_b_in=cmlp_b_in, cmlp_ln_g=cmlp_ln_g, cmlp_ln_b=cmlp_ln_b,
        cmlp_w_s=cmlp_w_s, cmlp_b_s=cmlp_b_s, cmlp_w_out=bf(cmlp_w_out),
        ssd_w_z=bf(ssd_w_in[:, :SSD_INNER]),
        ssd_w_xbc=bf(ssd_w_in[:, SSD_INNER:SSD_INNER + SSD_CONV_DIM]),
        ssd_w_dt=bf(jnp.pad(ssd_w_in[:, SSD_INNER + SSD_CONV_DIM:], ((0, 0), (0, dt_cols)))),
        ssd_conv_w=ssd_conv_w, ssd_conv_b=ssd_conv_b, ssd_dt_bias=ssd_dt_bias, ssd_a_log=ssd_a_log,
        ssd_d=ssd_d, ssd_norm_g=ssd_norm_g, ssd_w_out=bf(ssd_w_out),
        ln1_g=ln1_g, ln1_b=ln1_b, ln2_g=ln2_g, ln2_b=ln2_b,
        peer_w_q=bf(peer_w_q), peer_keys=bf(peer_keys), peer_u=bf(peer_u),
        peer_vt=bf(peer_v).reshape(DEPTH, -1, PEER_EC, D_MODEL).transpose(0, 1, 3, 2),
    )
    bp = x_prompt.shape[0]
    zeros = lambda *s: jnp.zeros(s, F32)
    (y_p, s5_re_p, s5_im_p, pool_p, _, conv_p, ssd_p) = _trunk(
        x_prompt, 0, zeros(bp, S5_GROUPS, S5_STATE), zeros(bp, S5_GROUPS, S5_STATE),
        zeros(bp, POOL_BUF, D_MODEL), zeros(bp, SSD_CONV - 1, SSD_CONV_DIM), None, W, False)
    past_len = 16384
    (y_s, s5_re_s, s5_im_s, pool_s, cmlp_v_s, conv_s, ssd_s) = _trunk(
        x_sample, past_len, state_s5_re, state_s5_im, state_pool, state_ssd_conv, state_ssd, W, True)
    return (y_p, y_s, s5_re_p, s5_im_p, pool_p, conv_p, ssd_p,
            s5_re_s, s5_im_s, pool_s, cmlp_v_s, conv_s, ssd_s)
```

```python
import functools
import math

import jax
import jax.numpy as jnp
from jax import lax
from jax.experimental import pallas as pl
from jax.experimental.pallas import tpu as pltpu

F32 = jnp.float32
BF16 = jnp.bfloat16

D_MODEL = 1024
DEPTH = 4
ALPHA = (2 * DEPTH) ** 0.25
LN_EPS = 1e-5
RMS_EPS = 1e-5

S5_GROUPS = 64
S5_GROUP = 16
S5_STATE = 64
S5_GB = 8
S5_ROWS = 1024

POOL_WINDOWS = (2, 4, 8, 16)
POOL_GROUP = 256
POOL_BUF = 15
POOL_HIST = 16

CMLP_WIDTH = 1024
CMLP_CHUNK = 128
CMLP_HEADS = 4
CMLP_HEAD_DIM = 256

SSD_INNER = 2048
SSD_HEAD_DIM = 64
SSD_HEADS = 32
SSD_STATE = 128
SSD_GROUPS = 4
SSD_HPG = SSD_HEADS // SSD_GROUPS
SSD_CONV = 4
SSD_CHUNK = 128
SSD_CONV_DIM = SSD_INNER + 2 * SSD_GROUPS * SSD_STATE
SSD_GN = SSD_GROUPS * SSD_STATE

PEER_HEADS = 8
PEER_NKEYS = 128
PEER_TOPK = 16
PEER_HALF = 128
PEER_ROUNDS = PEER_TOPK + 1
PEER_TQ = 256
PEER_TB = 512
PEER_EC = 1024

GELU_K0 = -2.0 * math.sqrt(2.0 / math.pi) * math.log2(math.e)
GELU_K1 = 0.044715 * GELU_K0

LANES = 128
SUBLANES = 8
VMEM_LIMIT_BYTES = 56 * 1024 * 1024


def _cparams(semantics, vmem=None):
    return pltpu.CompilerParams(dimension_semantics=semantics, vmem_limit_bytes=vmem)


def _ln(y, g, b):
    mu = jnp.mean(y, axis=-1, keepdims=True)
    yc = y - mu
    var = jnp.mean(yc * yc, axis=-1, keepdims=True)
    return yc * lax.rsqrt(var + LN_EPS) * g + b


def _gelu(x):
    w = x * (GELU_K0 + GELU_K1 * (x * x))
    return x / (1.0 + jnp.exp2(w))


def _dot(a, b):
    return jnp.dot(a, b, preferred_element_type=F32)


def _dot_nt(a, b):
    return lax.dot_general(a, b, (((1,), (1,)), ((), ())), preferred_element_type=F32)


def _dot_exact(a, b):
    return jnp.dot(a, b, preferred_element_type=F32, precision=lax.Precision.HIGHEST)


def _split_bf16(a):
    hi = a.astype(BF16)
    return hi, (a - hi.astype(F32)).astype(BF16)


def _dot_sel(a, sel):
    hi, lo = _split_bf16(a)
    return _dot(hi, sel) + _dot(lo, sel)


def _sel_dot(sel, b):
    hi, lo = _split_bf16(b)
    return _dot(sel, hi) + _dot(sel, lo)


def _mm_kernel(*refs, act, has_bias, has_gate):
    x_ref, w_ref = refs[0], refs[1]
    k = 2
    acc = _dot(x_ref[...].astype(BF16), w_ref[...])
    if has_bias:
        acc = acc + refs[k][...]
        k += 1
    if act == "gelu":
        acc = _gelu(acc)
    if has_gate:
        acc = refs[k][...] * jax.nn.sigmoid(acc)
        k += 1
    refs[k][...] = acc


def _mm(x, w, bias=None, act=None, gate=None, tm=1024, tn=1024):
    M, K = x.shape
    N = w.shape[1]
    tm, tn = min(tm, M), min(tn, N)
    assert M % tm == 0 and N % tn == 0
    in_specs = [pl.BlockSpec((tm, K), lambda i, j: (i, 0)),
                pl.BlockSpec((K, tn), lambda i, j: (0, j))]
    args = [x, w]
    if bias is not None:
        in_specs.append(pl.BlockSpec((1, tn), lambda i, j: (0, j)))
        args.append(bias.reshape(1, N))
    if gate is not None:
        in_specs.append(pl.BlockSpec((tm, tn), lambda i, j: (i, j)))
        args.append(gate)
    return pl.pallas_call(
        functools.partial(_mm_kernel, act=act, has_bias=bias is not None, has_gate=gate is not None),
        grid=(M // tm, N // tn),
        in_specs=in_specs,
        out_specs=pl.BlockSpec((tm, tn), lambda i, j: (i, j)),
        out_shape=jax.ShapeDtypeStruct((M, N), F32),
        compiler_params=_cparams(("parallel", "parallel"), VMEM_LIMIT_BYTES),
        name="mm",
    )(*args)


def _mm_ln_kernel(x_ref, w_ref, h_ref, g_ref, b_ref, o_ref):
    acc = _dot(x_ref[...].astype(BF16), w_ref[...])
    o_ref[...] = _ln(ALPHA * h_ref[...] + acc, g_ref[...], b_ref[...])


def _mm_ln(x, w, h, g, b, tm=512):
    M, K = x.shape
    N = w.shape[1]
    tm = min(tm, M)
    assert M % tm == 0
    return pl.pallas_call(
        _mm_ln_kernel,
        grid=(M // tm,),
        in_specs=[pl.BlockSpec((tm, K), lambda i: (i, 0)),
                  pl.BlockSpec((K, N), lambda i: (0, 0)),
                  pl.BlockSpec((tm, N), lambda i: (i, 0)),
                  pl.BlockSpec((1, N), lambda i: (0, 0)),
                  pl.BlockSpec((1, N), lambda i: (0, 0))],
        out_specs=pl.BlockSpec((tm, N), lambda i: (i, 0)),
        out_shape=jax.ShapeDtypeStruct((M, N), F32),
        compiler_params=_cparams(("parallel",), VMEM_LIMIT_BYTES),
        name="mm_ln",
    )(x, w, h, g.reshape(1, N), b.reshape(1, N))


def _s5_kernel(u_ref, wbb_ref, wc_ref, a_ref, d_ref, h0_ref, g_ref, hl_ref,
               h_scr, bu_scr, hs_scr, *, B, Tc):
    tc = pl.program_id(1)
    P = S5_GB * S5_STATE

    @pl.when(tc == 0)
    def _():
        h_scr[...] = h0_ref[0]

    u = u_ref[...]
    bu_scr[...] = _dot(u.astype(BF16), wbb_ref[0])
    ar = jnp.broadcast_to(a_ref[0, 0:1, :], (SUBLANES, P))
    ai = jnp.broadcast_to(a_ref[0, 1:2, :], (SUBLANES, P))

    def seq_body(sb, carry):
        r0 = pl.multiple_of(sb * SUBLANES, SUBLANES)

        def t_body(t, h):
            hr, hi = h
            rows = pl.ds(pl.multiple_of(t * B + r0, SUBLANES), SUBLANES)
            nr = ar * hr - ai * hi + bu_scr[rows, 0:P]
            ni = ar * hi + ai * hr + bu_scr[rows, P:2 * P]
            hs_scr[rows, 0:P] = nr
            hs_scr[rows, P:2 * P] = ni
            return nr, ni

        hr, hi = lax.fori_loop(0, Tc, t_body,
                               (h_scr[pl.ds(r0, SUBLANES), 0:P], h_scr[pl.ds(r0, SUBLANES), P:2 * P]))
        h_scr[pl.ds(r0, SUBLANES), 0:P] = hr
        h_scr[pl.ds(r0, SUBLANES), P:2 * P] = hi
        return carry

    lax.fori_loop(0, B // SUBLANES, seq_body, 0)
    y = _dot(hs_scr[...].astype(BF16), wc_ref[0]) + d_ref[...] * u
    g_ref[...] = _gelu(y)

    @pl.when(tc == pl.num_programs(1) - 1)
    def _():
        hl_ref[0] = h_scr[...]


def _s5_core(u_tm, h0_blk, wbb, wc, a_blk, d_skip, B, L):
    Tc = S5_ROWS // B
    assert B % SUBLANES == 0 and L % Tc == 0
    P2 = 2 * S5_GB * S5_STATE
    W = S5_GB * S5_GROUP
    n_gb = S5_GROUPS // S5_GB
    return pl.pallas_call(
        functools.partial(_s5_kernel, B=B, Tc=Tc),
        grid=(n_gb, L // Tc),
        in_specs=[pl.BlockSpec((S5_ROWS, W), lambda g, t: (t, g)),
                  pl.BlockSpec((1, W, P2), lambda g, t: (g, 0, 0)),
                  pl.BlockSpec((1, P2, W), lambda g, t: (g, 0, 0)),
                  pl.BlockSpec((1, 2, P2 // 2), lambda g, t: (g, 0, 0)),
                  pl.BlockSpec((1, W), lambda g, t: (0, g)),
                  pl.BlockSpec((1, B, P2), lambda g, t: (g, 0, 0))],
        out_specs=[pl.BlockSpec((S5_ROWS, W), lambda g, t: (t, g)),
                   pl.BlockSpec((1, B, P2), lambda g, t: (g, 0, 0))],
        out_shape=[jax.ShapeDtypeStruct((L * B, D_MODEL), F32),
                   jax.ShapeDtypeStruct((n_gb, B, P2), F32)],
        scratch_shapes=[pltpu.VMEM((B, P2), F32),
                        pltpu.VMEM((S5_ROWS, P2), F32),
                        pltpu.VMEM((S5_ROWS, P2), F32)],
        compiler_params=_cparams(("parallel", "arbitrary"), VMEM_LIMIT_BYTES),
        name="s5_core",
    )(u_tm, wbb, wc, a_blk, d_skip.reshape(1, D_MODEL), h0_blk)


def _s5_params(a_re, a_im, log_dt, b_re, b_im, c_re, c_im):
    dt = jnp.exp(log_dt)[:, None]
    mag = jnp.exp(a_re * dt)
    lb_r, lb_i = mag * jnp.cos(a_im * dt), mag * jnp.sin(a_im * dt)
    den = a_re * a_re + a_im * a_im
    f_r = ((lb_r - 1.0) * a_re + lb_i * a_im) / den
    f_i = (lb_i * a_re - (lb_r - 1.0) * a_im) / den
    bb_r = f_r[..., None] * b_re - f_i[..., None] * b_im
    bb_i = f_r[..., None] * b_im + f_i[..., None] * b_re
    n_gb = S5_GROUPS // S5_GB
    eye = jnp.eye(S5_GB, dtype=F32)

    def blockdiag_in(bb):
        t = bb.reshape(n_gb, S5_GB, S5_STATE, S5_GROUP).transpose(0, 1, 3, 2)
        return jnp.einsum("bgip,gh->bgihp", t, eye).reshape(n_gb, S5_GB * S5_GROUP, S5_GB * S5_STATE)

    def blockdiag_out(c):
        t = c.reshape(n_gb, S5_GB, S5_GROUP, S5_STATE).transpose(0, 1, 3, 2)
        return jnp.einsum("bgpi,gh->bgphi", t, eye).reshape(n_gb, S5_GB * S5_STATE, S5_GB * S5_GROUP)

    wbb = jnp.concatenate([blockdiag_in(bb_r), blockdiag_in(bb_i)], axis=-1).astype(BF16)
    wc = jnp.concatenate([blockdiag_out(c_re), -blockdiag_out(c_im)], axis=1).astype(BF16)
    a_blk = jnp.stack([lb_r.reshape(n_gb, -1), lb_i.reshape(n_gb, -1)], axis=1)
    return wbb, wc, a_blk


def _s5_state_to_blocks(h_re, h_im):
    B = h_re.shape[0]
    n_gb = S5_GROUPS // S5_GB
    r = h_re.reshape(B, n_gb, -1).transpose(1, 0, 2)
    i = h_im.reshape(B, n_gb, -1).transpose(1, 0, 2)
    return jnp.concatenate([r, i], axis=-1)


def _s5_blocks_to_state(h_blk):
    n_gb, B, P2 = h_blk.shape
    r = h_blk[..., :P2 // 2].transpose(1, 0, 2).reshape(B, S5_GROUPS, S5_STATE)
    i = h_blk[..., P2 // 2:].transpose(1, 0, 2).reshape(B, S5_GROUPS, S5_STATE)
    return r, i


def _causal_kernel(x_ref, st_ref, p1_ref, p2_ref, o_ref, ctx, *, mode, S, H, Tb, pos0, ntime):
    j = pl.program_id(2)
    HS, TS = H * S, Tb * S

    @pl.when(j == 0)
    def _():
        ctx[0:HS, :] = st_ref[0]

    ctx[HS:HS + TS, :] = x_ref[...]

    def back(k, lanes):
        return ctx[HS - k * S:HS - k * S + TS, lanes]

    if mode == "conv":
        acc = p2_ref[...] + back(0, slice(None)) * p1_ref[SSD_CONV - 1:SSD_CONV, :]
        for k in range(1, SSD_CONV):
            acc = acc + back(k, slice(None)) * p1_ref[SSD_CONV - 1 - k:SSD_CONV - k, :]
        o_ref[...] = acc * jax.nn.sigmoid(acc)
    else:
        for gi, w in enumerate(POOL_WINDOWS):
            lanes = slice(gi * POOL_GROUP, (gi + 1) * POOL_GROUP)
            cur = back(0, lanes)
            win = cur
            for k in range(1, w):
                win = win + back(k, lanes)
            if pos0 + 1 >= w:
                mean = win * (1.0 / w)
            else:
                assert S == 1, "position-dependent window counts need step-contiguous rows"
                t = lax.broadcasted_iota(jnp.int32, (TS, POOL_GROUP), 0)
                cnt = jnp.minimum(pos0 + j * Tb + t + 1, w).astype(F32)
                mean = win / cnt
            mixed = _dot((mean - cur).astype(BF16), p1_ref[gi])
            o_ref[:, lanes] = mixed * p2_ref[:, lanes]

    if ntime > 1:
        ctx[0:HS, :] = ctx[TS:TS + HS, :]


def _causal(x, st, p1, p2, *, mode, nseq, S, H, Tb, L, pos0):
    C = x.shape[1]
    ncol = C // D_MODEL
    ntime = L // Tb
    assert L % Tb == 0 and (H * S) % SUBLANES == 0
    if mode == "conv":
        p1_spec = pl.BlockSpec((SSD_CONV, D_MODEL), lambda s, c, j: (0, c))
        p2_spec = pl.BlockSpec((1, D_MODEL), lambda s, c, j: (0, c))
    else:
        p1_spec = pl.BlockSpec(p1.shape, lambda s, c, j: (0, 0, 0))
        p2_spec = pl.BlockSpec((1, D_MODEL), lambda s, c, j: (0, 0))
    return pl.pallas_call(
        functools.partial(_causal_kernel, mode=mode, S=S, H=H, Tb=Tb, pos0=pos0, ntime=ntime),
        grid=(nseq, ncol, ntime),
        in_specs=[pl.BlockSpec((Tb * S, D_MODEL), lambda s, c, j: (s * ntime + j, c)),
                  pl.BlockSpec((1, H * S, D_MODEL), lambda s, c, j: (s, 0, c)),
                  p1_spec, p2_spec],
        out_specs=pl.BlockSpec((Tb * S, D_MODEL), lambda s, c, j: (s * ntime + j, c)),
        out_shape=jax.ShapeDtypeStruct(x.shape, F32),
        scratch_shapes=[pltpu.VMEM(((H + Tb) * S, D_MODEL), F32)],
        compiler_params=_cparams(("parallel", "parallel", "arbitrary"), VMEM_LIMIT_BYTES),
        name="causal_" + mode,
    )(x, st, p1, p2)


def _causal_group(x, hist, p1, p2, *, mode, B, L, pos0, n_hist):
    C = x.shape[1]
    if L % 256 == 0:
        H = POOL_HIST if mode == "pool" else SUBLANES
        st = jnp.pad(hist, ((0, 0), (H - n_hist, 0), (0, 0)))
        return _causal(x, st, p1, p2, mode=mode, nseq=B, S=1, H=H, Tb=256, L=L, pos0=pos0)
    H = POOL_HIST if mode == "pool" else n_hist
    x_tm = x.reshape(B, L, C).transpose(1, 0, 2).reshape(L * B, C)
    st = jnp.pad(hist, ((0, 0), (H - n_hist, 0), (0, 0))).transpose(1, 0, 2).reshape(1, H * B, C)
    y = _causal(x_tm, st, p1, p2, mode=mode, nseq=1, S=B, H=H, Tb=L, L=L, pos0=pos0)
    return y.reshape(L, B, C).transpose(1, 0, 2).reshape(B * L, C)


def _cmlp_kernel(z_ref, ws_ref, bias_ref, lg_ref, lb_ref, *out_refs, nsub, write_v):
    o_ref = out_refs[0]
    for c in range(nsub):
        rows = slice(c * CMLP_CHUNK, (c + 1) * CMLP_CHUNK)
        v = _ln(z_ref[rows, CMLP_WIDTH:2 * CMLP_WIDTH], lg_ref[...], lb_ref[...])
        if write_v:
            out_refs[1][rows, :] = v
        vb = v.astype(BF16)
        for h in range(CMLP_HEADS):
            lanes = slice(h * CMLP_HEAD_DIM, (h + 1) * CMLP_HEAD_DIM)
            mixed = _dot(ws_ref[h], vb[:, lanes]) + bias_ref[:, lanes]
            o_ref[rows, lanes] = z_ref[rows, lanes] * mixed


def _cmlp_core(z, ws_eff, bias_eff, ln_g, ln_b, write_v):
    M = z.shape[0]
    tm = min(512, M)
    assert M % tm == 0 and tm % CMLP_CHUNK == 0
    out_shape = [jax.ShapeDtypeStruct((M, CMLP_WIDTH), F32)]
    out_specs = [pl.BlockSpec((tm, CMLP_WIDTH), lambda i: (i, 0))]
    if write_v:
        out_shape.append(jax.ShapeDtypeStruct((M, CMLP_WIDTH), F32))
        out_specs.append(pl.BlockSpec((tm, CMLP_WIDTH), lambda i: (i, 0)))
    return pl.pallas_call(
        functools.partial(_cmlp_kernel, nsub=tm // CMLP_CHUNK, write_v=write_v),
        grid=(M // tm,),
        in_specs=[pl.BlockSpec((tm, 2 * CMLP_WIDTH), lambda i: (i, 0)),
                  pl.BlockSpec((CMLP_HEADS, CMLP_CHUNK, CMLP_CHUNK), lambda i: (0, 0, 0)),
                  pl.BlockSpec((CMLP_CHUNK, CMLP_WIDTH), lambda i: (0, 0)),
                  pl.BlockSpec((1, CMLP_WIDTH), lambda i: (0, 0)),
                  pl.BlockSpec((1, CMLP_WIDTH), lambda i: (0, 0))],
        out_specs=out_specs,
        out_shape=out_shape,
        compiler_params=_cparams(("parallel",), VMEM_LIMIT_BYTES),
        name="cmlp_core",
    )(z, ws_eff, bias_eff, ln_g.reshape(1, -1), ln_b.reshape(1, -1))


def _cmlp_mix_params(w_s, b_s, L):
    q = min(L, CMLP_CHUNK)
    nrep = CMLP_CHUNK // q
    causal = jnp.tril(jnp.ones((q, q), dtype=bool))
    ws = jnp.where(causal[None], w_s[:, :q, :q], 0.0)
    eye = jnp.eye(nrep, dtype=F32)
    ws_eff = jnp.einsum("ab,hts->hatbs", eye, ws).reshape(CMLP_HEADS, CMLP_CHUNK, CMLP_CHUNK)
    bias = jnp.tile(b_s[:, :q], (1, nrep))
    bias_eff = jnp.repeat(bias.T, CMLP_HEAD_DIM, axis=1)
    return ws_eff.astype(BF16), bias_eff


def _ssd_kernel(*refs, nsub, Lc, carry, chunks_per_seq):
    if carry:
        (z_ref, xa_ref, dtr_ref, tril_ref, rm_ref, dtb_ref, a_ref, ex_ref, ext_ref, dsk_ref, ng_ref,
         y_ref, ho_ref, h_scr, acs_scr, xdt_scr, c_scr, b_scr, y_scr) = refs
        h0_ref = None
    else:
        (z_ref, xa_ref, dtr_ref, tril_ref, rm_ref, dtb_ref, a_ref, ex_ref, ext_ref, dsk_ref, ng_ref, h0_ref,
         y_ref, ho_ref, h_scr, acs_scr, xdt_scr, c_scr, b_scr, y_scr) = refs
    ci = pl.program_id(0)
    sub = pl.program_id(1)
    R = SSD_CHUNK
    GW = SSD_HPG * SSD_HEAD_DIM

    if carry:
        @pl.when(ci % chunks_per_seq == 0)
        def _():
            h_scr[...] = jnp.zeros_like(h_scr)

    @pl.when(sub == 0)
    def _():
        trilf = tril_ref[...]
        dt = jax.nn.softplus(dtr_ref[...] + dtb_ref[...])
        a_cs = _dot_exact(trilf, dt * a_ref[...])
        acs_scr[...] = a_cs
        a_cs_t = a_cs.T
        xs = xa_ref[:, 0:SSD_INNER]
        xdt = xs * _dot_sel(dt, ex_ref[...])
        xdt_scr[...] = xdt
        y_scr[...] = dsk_ref[...] * xs
        for g in range(SSD_GROUPS):
            bm = xa_ref[:, SSD_INNER + g * SSD_STATE:SSD_INNER + (g + 1) * SSD_STATE]
            cm = xa_ref[:, SSD_INNER + SSD_GN + g * SSD_STATE:SSD_INNER + SSD_GN + (g + 1) * SSD_STATE]
            b_scr[:, g * SSD_STATE:(g + 1) * SSD_STATE] = bm
            cmb = cm.astype(BF16)
            c_scr[:, g * SSD_STATE:(g + 1) * SSD_STATE] = cmb
            cb = _dot_nt(cmb, bm.astype(BF16)) * trilf
            for jh in range(SSD_HPG):
                hd = g * SSD_HPG + jh
                lanes = slice(hd * SSD_HEAD_DIM, (hd + 1) * SSD_HEAD_DIM)
                seg = a_cs[:, hd:hd + 1] - a_cs_t[hd:hd + 1, :]
                lmat = (cb * jnp.exp(jnp.minimum(seg, 0.0))).astype(BF16)
                y_scr[:, lanes] += _dot(lmat, xdt[:, lanes].astype(BF16))

    rm = rm_ref[sub]
    a_cs = acs_scr[...]
    a_last = acs_scr[pl.ds(sub * Lc + Lc - 1, 1), :]
    e_in = _dot_sel(jnp.exp(a_cs) * rm, ex_ref[...])
    d_end = jnp.exp(jnp.minimum(a_last - a_cs, 0.0)) * rm
    xw = xdt_scr[...] * _dot_sel(d_end, ex_ref[...])
    cd = _sel_dot(ext_ref[...], jnp.broadcast_to(jnp.exp(a_last), (R, SSD_STATE)).T)
    for g in range(SSD_GROUPS):
        rows = slice(g * GW, (g + 1) * GW)
        if carry:
            hg = h_scr[rows, :]
        else:
            hg = h0_ref[0, rows, :]
        yoff = _dot_nt(c_scr[:, g * SSD_STATE:(g + 1) * SSD_STATE], hg.astype(BF16))
        y_scr[:, rows] += e_in[:, rows] * yoff
        st = _dot(xw[:, rows].T.astype(BF16), b_scr[:, g * SSD_STATE:(g + 1) * SSD_STATE].astype(BF16))
        hn = hg * cd[rows, :] + st
        if carry:
            h_scr[rows, :] = hn
        else:
            ho_ref[0, rows, :] = hn

    if carry:
        @pl.when(ci % chunks_per_seq == chunks_per_seq - 1)
        def _():
            ho_ref[0] = h_scr[...]

    @pl.when(sub == nsub - 1)
    def _():
        z = z_ref[...]
        yg = y_scr[...] * (z * jax.nn.sigmoid(z))
        for g in range(SSD_GROUPS):
            lanes = slice(g * GW, (g + 1) * GW)
            t = yg[:, lanes]
            r = lax.rsqrt(jnp.mean(t * t, axis=-1, keepdims=True) + RMS_EPS)
            y_ref[:, lanes] = t * r * ng_ref[:, lanes]


def _ssd_core(z, xa, dtr, h0, dt_bias, a_log, d_skip, norm_g, B, L):
    M = z.shape[0]
    R = SSD_CHUNK
    carry = L >= R
    if carry:
        assert L % R == 0 and h0 is None
        nsub, Lc, cps = 1, R, L // R
    else:
        assert R % L == 0 and M % R == 0
        nsub, Lc, cps = R // L, L, 1
    nchunk = M // R
    r = jnp.arange(R)
    same = (r[:, None] // Lc) == (r[None, :] // Lc)
    tril = (same & (r[None, :] <= r[:, None])).astype(F32)
    rm = ((r[None, :, None] // Lc) == jnp.arange(nsub)[:, None, None]).astype(F32)
    rm = jnp.broadcast_to(rm, (nsub, R, SSD_STATE))
    hsel = (jnp.arange(SSD_STATE)[:, None] == (jnp.arange(SSD_INNER)[None, :] // SSD_HEAD_DIM)).astype(F32)
    pad = SSD_STATE - SSD_HEADS
    dtb = jnp.pad(dt_bias, (0, pad)).reshape(1, SSD_STATE)
    a = jnp.pad(-jnp.exp(a_log), (0, pad)).reshape(1, SSD_STATE)
    dsk = jnp.repeat(d_skip, SSD_HEAD_DIM).reshape(1, SSD_INNER)
    HS = SSD_HEADS * SSD_HEAD_DIM
    const = lambda *shape: pl.BlockSpec(shape, lambda c, s: (0,) * len(shape))
    in_specs = [pl.BlockSpec((R, SSD_INNER), lambda c, s: (c, 0)),
                pl.BlockSpec((R, SSD_CONV_DIM), lambda c, s: (c, 0)),
                pl.BlockSpec((R, SSD_STATE), lambda c, s: (c, 0)),
                const(R, R), const(nsub, R, SSD_STATE), const(1, SSD_STATE), const(1, SSD_STATE),
                const(SSD_STATE, SSD_INNER), const(SSD_INNER, SSD_STATE), const(1, SSD_INNER),
                const(1, SSD_INNER)]
    args = [z, xa, dtr, tril, rm, dtb, a, hsel.astype(BF16), hsel.T.astype(BF16), dsk,
            norm_g.reshape(1, SSD_INNER)]
    if carry:
        ho_spec = pl.BlockSpec((1, HS, SSD_STATE), lambda c, s: (c // cps, 0, 0))
    else:
        in_specs.append(pl.BlockSpec((1, HS, SSD_STATE), lambda c, s: (c * nsub + s, 0, 0)))
        args.append(h0.reshape(B, HS, SSD_STATE))
        ho_spec = pl.BlockSpec((1, HS, SSD_STATE), lambda c, s: (c * nsub + s, 0, 0))
    return pl.pallas_call(
        functools.partial(_ssd_kernel, nsub=nsub, Lc=Lc, carry=carry, chunks_per_seq=cps),
        grid=(nchunk, nsub),
        in_specs=in_specs,
        out_specs=[pl.BlockSpec((R, SSD_INNER), lambda c, s: (c, 0)), ho_spec],
        out_shape=[jax.ShapeDtypeStruct((M, SSD_INNER), F32),
                   jax.ShapeDtypeStruct((B, HS, SSD_STATE), F32)],
        scratch_shapes=[pltpu.VMEM((HS, SSD_STATE), F32),
                        pltpu.VMEM((R, SSD_STATE), F32),
                        pltpu.VMEM((R, SSD_INNER), F32),
                        pltpu.VMEM((R, SSD_GN), BF16),
                        pltpu.VMEM((R, SSD_GN), F32),
                        pltpu.VMEM((R, SSD_INNER), F32)],
        compiler_params=_cparams(("arbitrary", "arbitrary"), VMEM_LIMIT_BYTES),
        name="ssd_core",
    )(*args)


def _compare_exchange(xs, i, j):
    a, b = xs[i], xs[j]
    xs[i], xs[j] = jnp.maximum(a, b), jnp.minimum(a, b)


def _bitonic_sort_desc(xs):
    n = len(xs)
    k = 2
    while k <= n:
        j = k // 2
        while j >= 1:
            for i in range(n):
                p = i ^ j
                if p > i:
                    if (i & k) == 0:
                        _compare_exchange(xs, i, p)
                    else:
                        _compare_exchange(xs, p, i)
            j //= 2
        k *= 2


def _bitonic_merge_desc(xs):
    j = len(xs) // 2
    while j >= 1:
        for i in range(len(xs)):
            p = i ^ j
            if p > i:
                _compare_exchange(xs, i, p)
        j //= 2


def _top16_of_128(s):
    xs = [s[SUBLANES * i:SUBLANES * (i + 1), :] for i in range(PEER_TOPK)]
    _bitonic_sort_desc(xs)
    dropped = jnp.full(xs[0].shape, -jnp.inf, F32)
    for shift in (4, 2, 1):
        ys = [pltpu.roll(xs[PEER_TOPK - 1 - i], shift, 0) for i in range(PEER_TOPK)]
        lost = functools.reduce(jnp.maximum, [jnp.minimum(a, b) for a, b in zip(xs, ys)])
        dropped = jnp.maximum(jnp.maximum(dropped, pltpu.roll(dropped, shift, 0)), lost)
        xs = [jnp.maximum(a, b) for a, b in zip(xs, ys)]
        _bitonic_merge_desc(xs)
    return xs, dropped


def _peer_route_kernel(x_ref, wq_ref, keys_ref, r2_ref, e2_ref, n1_ref, c1_ref, s_scr, m_scr):
    Tq = x_ref.shape[0]
    NEG = -jnp.inf
    q = _dot(x_ref[...].astype(BF16), wq_ref[...]).astype(BF16)
    for h in range(PEER_HEADS):
        for half in range(2):
            col = (h * 2 + half) * PEER_HALF
            s = _dot_nt(keys_ref[h, half], q[:, col:col + PEER_HALF])
            s_scr[half, h] = s
            top, nxt = _top16_of_128(s)
            for r in range(PEER_TOPK):
                m_scr[half, r, h:h + 1, :] = top[r][0:1, :]
            m_scr[half, PEER_TOPK, h:h + 1, :] = nxt[0:1, :]

    pairs = [(r, c) for r in range(PEER_ROUNDS) for c in range(PEER_ROUNDS)
             if (r + 1) * (c + 1) <= PEER_ROUNDS]
    sums = [m_scr[0, r] + m_scr[1, c] for r, c in pairs]
    cum = jnp.zeros((PEER_HEADS, Tq), F32)
    v16 = jnp.full((PEER_HEADS, Tq), NEG, F32)
    v17 = jnp.full((PEER_HEADS, Tq), NEG, F32)
    for _ in range(PEER_ROUNDS):
        mx = functools.reduce(jnp.maximum, sums)
        hit = [sv == mx for sv in sums]
        new = cum + functools.reduce(lambda p, t: p + t, [jnp.where(hm, 1.0, 0.0) for hm in hit])
        v16 = jnp.where((cum < PEER_TOPK) & (new >= PEER_TOPK), mx, v16)
        v17 = jnp.where((cum < PEER_TOPK + 1) & (new >= PEER_TOPK + 1), mx, v17)
        sums = [jnp.where(hm, NEG, sv) for hm, sv in zip(hit, sums)]
        cum = new
    tau = 0.5 * (v16 + v17)
    m1, m2 = m_scr[0, 0], m_scr[1, 0]
    zsum = jnp.zeros((PEER_HEADS, Tq), F32)
    for r, c in pairs:
        if r < PEER_TOPK and c < PEER_TOPK:
            a1, a2 = m_scr[0, r], m_scr[1, c]
            zsum = zsum + jnp.where(a2 >= tau - a1, jnp.exp(a1 - m1) * jnp.exp(a2 - m2), 0.0)
    zinv = 1.0 / zsum
    for h in range(PEER_HEADS):
        s1, s2 = s_scr[0, h], s_scr[1, h]
        thr = tau[h:h + 1, :] - s1
        rank2 = jnp.zeros((PEER_NKEYS, Tq), F32)
        count1 = jnp.zeros((PEER_NKEYS, Tq), F32)
        for r in range(PEER_TOPK):
            a2 = m_scr[1, r, h:h + 1, :]
            rank2 = jnp.where(a2 > s2, r + 1.0, rank2)
            count1 = jnp.where(a2 >= thr, r + 1.0, count1)
        r2_ref[h] = rank2
        n1_ref[h] = count1
        c1_ref[h] = jnp.exp(s1 - m1[h:h + 1, :]) * zinv[h:h + 1, :]
        e2_ref[h] = jnp.exp(s2 - m2[h:h + 1, :])


def _peer_route(hn, wq, keys):
    T = hn.shape[0]
    Tq = min(PEER_TQ, T)
    assert T % Tq == 0
    shape = (PEER_HEADS, PEER_NKEYS, T)
    ospec = pl.BlockSpec((PEER_HEADS, PEER_NKEYS, Tq), lambda i: (0, 0, i))
    return pl.pallas_call(
        _peer_route_kernel,
        grid=(T // Tq,),
        in_specs=[pl.BlockSpec((Tq, D_MODEL), lambda i: (i, 0)),
                  pl.BlockSpec(wq.shape, lambda i: (0, 0)),
                  pl.BlockSpec(keys.shape, lambda i: (0, 0, 0, 0))],
        out_specs=[ospec] * 4,
        out_shape=[jax.ShapeDtypeStruct(shape, F32)] * 4,
        scratch_shapes=[pltpu.VMEM((2, PEER_HEADS, PEER_NKEYS, Tq), F32),
                        pltpu.VMEM((2, PEER_ROUNDS, PEER_HEADS, Tq), F32)],
        compiler_params=_cparams(("parallel",), VMEM_LIMIT_BYTES),
        name="peer_route",
    )(hn, wq, keys)


def _peer_ffn_kernel(x_ref, u_ref, vt_ref, r2_ref, e2_ref, n1_ref, c1_ref, g_ref, b_ref,
                     o_ref, xb_scr, s_scr, p_scr, acc_scr, r2b_scr, e2b_scr):
    c = pl.program_id(1)
    Tb = x_ref.shape[0]
    ni = PEER_EC // PEER_NKEYS
    assert ni == SUBLANES
    bf16_rows = 2 * SUBLANES

    @pl.when(c == 0)
    def _():
        xb_scr[...] = x_ref[...].astype(BF16)
        acc_scr[...] = jnp.zeros_like(acc_scr)
        for h in range(PEER_HEADS):
            r2b_scr[h] = r2_ref[h].astype(BF16)
            e2b_scr[h] = e2_ref[h].astype(BF16)

    half = PEER_EC // 2
    s_scr[0:half, :] = _dot_nt(u_ref[0:half, :], xb_scr[...])
    s_scr[half:, :] = _dot_nt(u_ref[half:, :], xb_scr[...])

    def row_tile(rows, ii):
        r = jnp.broadcast_to(rows[ii:ii + 1, :], (bf16_rows, LANES)).astype(BF16)
        return jnp.tile(r, (PEER_NKEYS // bf16_rows, 1))

    for tt in range(Tb // LANES):
        lanes = slice(tt * LANES, (tt + 1) * LANES)
        n1 = [n1_ref[h, :, lanes] for h in range(PEER_HEADS)]
        c1 = [c1_ref[h, :, lanes] for h in range(PEER_HEADS)]
        for ii in range(ni):
            rows = slice(ii * PEER_NKEYS, (ii + 1) * PEER_NKEYS)
            gate = jnp.zeros((PEER_NKEYS, LANES), BF16)
            for h in range(PEER_HEADS):
                sel = r2b_scr[h, :, lanes] < row_tile(n1[h], ii)
                gate = gate + jnp.where(sel, e2b_scr[h, :, lanes] * row_tile(c1[h], ii), 0)
            p_scr[rows, lanes] = _gelu(s_scr[rows, lanes]).astype(BF16) * gate
    acc_scr[...] += (_dot(vt_ref[0, :, 0:half], p_scr[0:half, :])
                     + _dot(vt_ref[0, :, half:], p_scr[half:, :]))

    @pl.when(c == pl.num_programs(1) - 1)
    def _():
        o_ref[...] = _ln(ALPHA * x_ref[...] + acc_scr[...].T, g_ref[...], b_ref[...])


def _peer_ffn(hn, u, vt, r2, e2, n1, c1, g, b):
    T = hn.shape[0]
    E = u.shape[0]
    Tb = min(PEER_TB, T)
    assert T % Tb == 0 and vt.shape == (E // PEER_EC, D_MODEL, PEER_EC)
    jspec = pl.BlockSpec((PEER_HEADS, PEER_NKEYS, Tb), lambda i, c: (0, 0, i))
    ispec = pl.BlockSpec((PEER_HEADS, PEER_EC // PEER_NKEYS, Tb), lambda i, c: (0, c, i))
    return pl.pallas_call(
        _peer_ffn_kernel,
        grid=(T // Tb, E // PEER_EC),
        in_specs=[pl.BlockSpec((Tb, D_MODEL), lambda i, c: (i, 0)),
                  pl.BlockSpec((PEER_EC, D_MODEL), lambda i, c: (c, 0)),
                  pl.BlockSpec((1, D_MODEL, PEER_EC), lambda i, c: (c, 0, 0)),
                  jspec, jspec, ispec, ispec,
                  pl.BlockSpec((1, D_MODEL), lambda i, c: (0, 0)),
                  pl.BlockSpec((1, D_MODEL), lambda i, c: (0, 0))],
        out_specs=pl.BlockSpec((Tb, D_MODEL), lambda i, c: (i, 0)),
        out_shape=jax.ShapeDtypeStruct((T, D_MODEL), F32),
        scratch_shapes=[pltpu.VMEM((Tb, D_MODEL), BF16),
                        pltpu.VMEM((PEER_EC, Tb), F32),
                        pltpu.VMEM((PEER_EC, Tb), BF16),
                        pltpu.VMEM((D_MODEL, Tb), F32),
                        pltpu.VMEM((PEER_HEADS, PEER_NKEYS, Tb), BF16),
                        pltpu.VMEM((PEER_HEADS, PEER_NKEYS, Tb), BF16)],
        compiler_params=_cparams(("parallel", "arbitrary"), VMEM_LIMIT_BYTES),
        name="peer_ffn",
    )(hn, u, vt, r2, e2, n1, c1, g.reshape(1, -1), b.reshape(1, -1))


def _trunk(x, pos0, s5_re, s5_im, pool_buf, conv_buf, ssd_h, W, need_v):
    B, L, _ = x.shape
    T = B * L
    h = x.reshape(T, D_MODEL)
    to_tm = lambda a: a.reshape(B, L, -1).transpose(1, 0, 2).reshape(T, -1)
    to_bm = lambda a: a.reshape(L, B, -1).transpose(1, 0, 2).reshape(T, -1)
    cmlp_v = None
    for i in range(DEPTH):
        kind = i % 4
        if kind == 0:
            u = _mm(h, W["s5_w_in"])
            g_tm, hl = _s5_core(to_tm(u), _s5_state_to_blocks(s5_re, s5_im), W["s5_wbb"], W["s5_wc"],
                                W["s5_a"], W["s5_d"], B, L)
            s5_re, s5_im = _s5_blocks_to_state(hl)
            g = to_bm(g_tm)
            mid = _mm(g, W["s5_w_glu"], bias=W["s5_b_glu"], gate=g)
            w_out = W["s5_w_out"]
        elif kind == 1:
            u = _mm(h, W["pool_w_in"])
            mid = _causal_group(u, pool_buf, W["pool_w_grp"], W["pool_scale"].reshape(1, -1),
                                mode="pool", B=B, L=L, pos0=pos0, n_hist=POOL_BUF)
            pool_buf = jnp.concatenate([pool_buf, u.reshape(B, L, -1)], axis=1)[:, -POOL_BUF:]
            w_out = W["pool_w_out"]
        elif kind == 2:
            z = _mm(h, W["cmlp_w_in"], bias=W["cmlp_b_in"], act="gelu")
            ws_eff, bias_eff = _cmlp_mix_params(W["cmlp_w_s"], W["cmlp_b_s"], L)
            outs = _cmlp_core(z, ws_eff, bias_eff, W["cmlp_ln_g"], W["cmlp_ln_b"], write_v=need_v)
            mid = outs[0]
            if need_v:
                cmlp_v = outs[1].reshape(B, L, CMLP_WIDTH)
            w_out = W["cmlp_w_out"]
        else:
            zg = _mm(h, W["ssd_w_z"])
            xbc = _mm(h, W["ssd_w_xbc"])
            dtr = _mm(h, W["ssd_w_dt"])
            xa = _causal_group(xbc, conv_buf, W["ssd_conv_w"], W["ssd_conv_b"].reshape(1, -1),
                               mode="conv", B=B, L=L, pos0=pos0, n_hist=SSD_CONV - 1)
            conv_buf = jnp.concatenate([conv_buf, xbc.reshape(B, L, -1)], axis=1)[:, -(SSD_CONV - 1):]
            mid, hs = _ssd_core(zg, xa, dtr, ssd_h, W["ssd_dt_bias"], W["ssd_a_log"], W["ssd_d"],
                                W["ssd_norm_g"], B, L)
            ssd_h = hs.reshape(B, SSD_HEADS, SSD_HEAD_DIM, SSD_STATE)
            w_out = W["ssd_w_out"]
        h = _mm_ln(mid, w_out, h, W["ln1_g"][i], W["ln1_b"][i])
        r2, e2, n1, c1 = _peer_route(h, W["peer_w_q"][i], W["peer_keys"][i])
        h = _peer_ffn(h, W["peer_u"][i], W["peer_vt"][i], r2, e2, n1, c1, W["ln2_g"][i], W["ln2_b"][i])
    return h.reshape(B, L, D_MODEL), s5_re, s5_im, pool_buf, cmlp_v, conv_buf, ssd_h


def kernel(x_prompt, x_sample, state_s5_re, state_s5_im, state_pool, state_ssd_conv, state_ssd, s5_w_in, s5_a_re, s5_a_im, s5_log_dt, s5_b_re, s5_b_im, s5_c_re, s5_c_im, s5_d, s5_w_glu, s5_b_glu, s5_w_out, pool_w_in, pool_w_grp, pool_scale, pool_w_out, cmlp_w_in, cmlp_b_in, cmlp_ln_g, cmlp_ln_b, cmlp_w_s, cmlp_b_s, cmlp_w_out, ssd_w_in, ssd_conv_w, ssd_conv_b, ssd_dt_bias, ssd_a_log, ssd_d, ssd_norm_g, ssd_w_out, ln1_g, ln1_b, ln2_g, ln2_b, peer_w_q, peer_keys, peer_u, peer_v):
    bf = lambda a: a.astype(BF16)
    wbb, wc, a_blk = _s5_params(s5_a_re, s5_a_im, s5_log_dt, s5_b_re, s5_b_im, s5_c_re, s5_c_im)
    dt_cols = SSD_STATE - SSD_HEADS
    W = dict(
        s5_w_in=bf(s5_w_in), s5_wbb=wbb, s5_wc=wc, s5_a=a_blk, s5_d=s5_d,
        s5_w_glu=bf(s5_w_glu), s5_b_glu=s5_b_glu, s5_w_out=bf(s5_w_out),
        pool_w_in=bf(pool_w_in), pool_w_grp=bf(pool_w_grp), pool_scale=pool_scale, pool_w_out=bf(pool_w_out),
        cmlp_w_in=bf(cmlp_w_in), cmlp_b_in=cmlp_b_in, cmlp_ln_g=cmlp_ln_g, cmlp_ln_b=cmlp_ln_b,
        cmlp_w_s=cmlp_w_s, cmlp_b_s=cmlp_b_s, cmlp_w_out=bf(cmlp_w_out),
        ssd_w_z=bf(ssd_w_in[:, :SSD_INNER]),
        ssd_w_xbc=bf(ssd_w_in[:, SSD_INNER:SSD_INNER + SSD_CONV_DIM]),
        ssd_w_dt=bf(jnp.pad(ssd_w_in[:, SSD_INNER + SSD_CONV_DIM:], ((0, 0), (0, dt_cols)))),
        ssd_conv_w=ssd_conv_w, ssd_conv_b=ssd_conv_b, ssd_dt_bias=ssd_dt_bias, ssd_a_log=ssd_a_log,
        ssd_d=ssd_d, ssd_norm_g=ssd_norm_g, ssd_w_out=bf(ssd_w_out),
        ln1_g=ln1_g, ln1_b=ln1_b, ln2_g=ln2_g, ln2_b=ln2_b,
        peer_w_q=bf(peer_w_q), peer_keys=bf(peer_keys), peer_u=bf(peer_u),
        peer_vt=bf(peer_v).reshape(DEPTH, -1, PEER_EC, D_MODEL).transpose(0, 1, 3, 2),
    )
    bp = x_prompt.shape[0]
    zeros = lambda *s: jnp.zeros(s, F32)
    (y_p, s5_re_p, s5_im_p, pool_p, _, conv_p, ssd_p) = _trunk(
        x_prompt, 0, zeros(bp, S5_GROUPS, S5_STATE), zeros(bp, S5_GROUPS, S5_STATE),
        zeros(bp, POOL_BUF, D_MODEL), zeros(bp, SSD_CONV - 1, SSD_CONV_DIM), None, W, False)
    past_len = 16384
    (y_s, s5_re_s, s5_im_s, pool_s, cmlp_v_s, conv_s, ssd_s) = _trunk(
        x_sample, past_len, state_s5_re, state_s5_im, state_pool, state_ssd_conv, state_ssd, W, True)
    return (y_p, y_s, s5_re_p, s5_im_p, pool_p, conv_p, ssd_p,
            s5_re_s, s5_im_s, pool_s, cmlp_v_s, conv_s, ssd_s)
```
